```python
import math
import jax, jax.numpy as jnp
from jax import lax
import numpy as np

D_MODEL = 1024
BATCH = 4
SEQ = 8192
DEPTH = 1

CHUNK = 64
CONV_K = 4
EPS = 1e-6
ML_HEADS = 8
ML_INNER = 2 * D_MODEL
ML_DV = ML_INNER // ML_HEADS
ML_DQK = ML_DV // 2
ML_QK2 = 2 * ML_HEADS * ML_DQK
SSM_INNER = 2 * D_MODEL
SSM_HEADDIM = 64
SSM_HEADS = SSM_INNER // SSM_HEADDIM
SSM_GROUPS = 4
SSM_STATE = 128
SSM_XBC = SSM_INNER + 2 * SSM_GROUPS * SSM_STATE
SEG_SIZES = (ML_QK2, ML_INNER, ML_INNER, ML_INNER, ML_HEADS, ML_HEADS,
             SSM_XBC, SSM_INNER, SSM_HEADS, 2 * D_MODEL)
IN_WIDTH = sum(SEG_SIZES)
SEG_SPLITS = tuple(np.cumsum(SEG_SIZES)[:-1].tolist())

kernel_name = 'hybrid_mlstm_ssd_gated_merge'


def _rmsnorm(x, w):
    xf = x.astype(jnp.float32)
    y = xf * lax.rsqrt(jnp.mean(xf * xf, axis=-1, keepdims=True) + EPS)
    return (y * w.astype(jnp.float32)).astype(x.dtype)


def _causal_dwconv(x, w, b):
    s = x.shape[1]
    k = w.shape[0]
    xp = jnp.pad(x, ((0, 0), (k - 1, 0), (0, 0)))
    y = b
    for j in range(k):
        y = y + xp[:, j:j + s, :] * w[j]
    return y


def _chunk_heads(t):
    b, s = t.shape[:2]
    t = t.reshape(b, s // CHUNK, CHUNK, *t.shape[2:])
    perm = (1, 0, 3, 2) + tuple(range(4, t.ndim))
    return jnp.transpose(t, perm)


def _mlstm(q, k, v, i_pre, f_pre):
    f32 = jnp.float32
    b, s, h, dqk = q.shape
    dv = v.shape[-1]
    qc = _chunk_heads(q.astype(f32) * (dqk ** -0.5))
    kc = _chunk_heads(k.astype(f32))
    vc = _chunk_heads(v.astype(f32))
    ic = _chunk_heads(i_pre.astype(f32))
    fc = _chunk_heads(jax.nn.log_sigmoid(f_pre.astype(f32)))
    causal = jnp.tril(jnp.ones((CHUNK, CHUNK), dtype=bool))

    def step(carry, inp):
        c_mat, n_vec, m = carry
        qb, kb, vb, ib, fb = inp
        a = jnp.cumsum(fb, axis=-1)
        dmat = jnp.where(causal, a[..., :, None] - a[..., None, :] + ib[..., None, :], -jnp.inf)
        inter = a + m[..., None]
        m_t = jnp.maximum(inter, jnp.max(dmat, axis=-1))
        w_intra = jnp.exp(dmat - m_t[..., None])
        w_inter = jnp.exp(inter - m_t)
        sc = jnp.einsum('bhld,bhsd->bhls', qb, kb) * w_intra
        num = jnp.einsum('bhls,bhsv->bhlv', sc, vb) + w_inter[..., None] * jnp.einsum('bhld,bhdv->bhlv', qb, c_mat)
        den = jnp.sum(sc, axis=-1) + w_inter * jnp.einsum('bhld,bhd->bhl', qb, n_vec)
        hb = num / jnp.maximum(jnp.abs(den), jnp.exp(-m_t))[..., None]
        a_last = a[..., -1]
        g = a_last[..., None] - a + ib
        m_new = jnp.maximum(a_last + m, jnp.max(g, axis=-1))
        wk = jnp.exp(g - m_new[..., None])
        decay = jnp.exp(a_last + m - m_new)
        c_new = decay[..., None, None] * c_mat + jnp.einsum('bhl,bhld,bhlv->bhdv', wk, kb, vb)
        n_new = decay[..., None] * n_vec + jnp.einsum('bhl,bhld->bhd', wk, kb)
        return (c_new, n_new, m_new), hb

    init = (jnp.zeros((b, h, dqk, dv), f32), jnp.zeros((b, h, dqk), f32), jnp.zeros((b, h), f32))
    _, hs = lax.scan(step, init, (qc, kc, vc, ic, fc))
    return jnp.transpose(hs, (1, 0, 3, 2, 4)).reshape(b, s, h, dv)


def _ssd(x, dt, a_coef, bm, cm):
    f32 = jnp.float32
    b, s, h, p = x.shape
    g, n = bm.shape[2:]
    hg = h // g
    nc = s // CHUNK
    xc = _chunk_heads(x.astype(f32)).reshape(nc, b, g, hg, CHUNK, p)
    dtc = _chunk_heads(dt.astype(f32)).reshape(nc, b, g, hg, CHUNK)
    bc = _chunk_heads(bm.astype(f32))
    cc = _chunk_heads(cm.astype(f32))
    a_g = a_coef.astype(f32).reshape(g, hg)
    causal = jnp.tril(jnp.ones((CHUNK, CHUNK), dtype=bool))

    def step(state, inp):
        xb, dtb, bb, cb = inp
        a = jnp.cumsum(dtb * a_g[..., None], axis=-1)
        decay = jnp.exp(jnp.where(causal, a[..., :, None] - a[..., None, :], -jnp.inf))
        scores = jnp.einsum('bgln,bgsn->bgls', cb, bb)
        mix = decay * scores[:, :, None] * dtb[..., None, :]
        y = jnp.einsum('bghls,bghsp->bghlp', mix, xb)
        y = y + jnp.exp(a)[..., None] * jnp.einsum('bgln,bghpn->bghlp', cb, state)
        a_last = a[..., -1:]
        wts = jnp.exp(a_last - a) * dtb
        state = jnp.exp(a_last)[..., None] * state + jnp.einsum('bghs,bgsn,bghsp->bghpn', wts, bb, xb)
        return state, y

    init = jnp.zeros((b, g, hg, p, n), f32)
    _, ys = lax.scan(step, init, (xc, dtc, bc, cc))
    return jnp.transpose(ys, (1, 0, 4, 2, 3, 5)).reshape(b, s, h, p)


def setup_inputs(seed: int = 0) -> dict:
    key = jax.random.key(seed)
    ks = jax.random.split(key, 24)
    f32 = jnp.float32

    def nrm(k, shape, scale):
        return jax.random.normal(k, shape, f32) * scale

    x = nrm(ks[0], (BATCH, SEQ, D_MODEL), 1.0)
    c = nrm(ks[1], (BATCH, D_MODEL), 1.0)
    norm_w = 1.0 + nrm(ks[2], (DEPTH, D_MODEL), 0.02)
    ada_w = nrm(ks[3], (DEPTH, D_MODEL, 3 * D_MODEL), 0.1 * D_MODEL ** -0.5)
    ada_b = nrm(ks[4], (DEPTH, 3 * D_MODEL), 0.02)
    w_in = nrm(ks[5], (DEPTH, D_MODEL, IN_WIDTH), D_MODEL ** -0.5)
    b_base = nrm(ks[6], (DEPTH, IN_WIDTH), 0.02)
    f_bias = jnp.linspace(3.0, 6.0, ML_HEADS, dtype=f32) + nrm(ks[7], (DEPTH, ML_HEADS), 0.1)
    dt0 = jnp.exp(jax.random.uniform(ks[8], (DEPTH, SSM_HEADS), f32, math.log(1e-3), math.log(1e-1)))
    dt_bias = dt0 + jnp.log(-jnp.expm1(-dt0))
    f_off = sum(SEG_SIZES[:5])
    dt_off = sum(SEG_SIZES[:8])
    b_in = b_base.at[:, f_off:f_off + ML_HEADS].set(f_bias).at[:, dt_off:dt_off + SSM_HEADS].set(dt_bias)
    ml_conv_w = nrm(ks[9], (DEPTH, CONV_K, ML_QK2), CONV_K ** -0.5)
    ml_conv_b = nrm(ks[10], (DEPTH, ML_QK2), 0.02)
    ml_norm_w = 1.0 + nrm(ks[11], (DEPTH, ML_INNER), 0.02)
    ssm_conv_w = nrm(ks[12], (DEPTH, CONV_K, SSM_XBC), CONV_K ** -0.5)
    ssm_conv_b = nrm(ks[13], (DEPTH, SSM_XBC), 0.02)
    ssm_a_log = jnp.log(jax.random.uniform(ks[14], (DEPTH, SSM_HEADS), f32, 1.0, 16.0))
    ssm_d = 1.0 + nrm(ks[15], (DEPTH, SSM_HEADS), 0.1)
    ssm_norm_w = 1.0 + nrm(ks[16], (DEPTH, SSM_INNER), 0.02)
    w_proj_m = nrm(ks[17], (DEPTH, ML_INNER, D_MODEL), ML_INNER ** -0.5)
    w_proj_s = nrm(ks[18], (DEPTH, SSM_INNER, D_MODEL), SSM_INNER ** -0.5)
    w_out = nrm(ks[19], (DEPTH, D_MODEL, D_MODEL), D_MODEL ** -0.5)
    final_w = 1.0 + nrm(ks[20], (D_MODEL,), 0.02)
    return {'x': x, 'c': c, 'norm_w': norm_w, 'ada_w': ada_w, 'ada_b': ada_b,
            'w_in': w_in, 'b_in': b_in, 'ml_conv_w': ml_conv_w, 'ml_conv_b': ml_conv_b,
            'ml_norm_w': ml_norm_w, 'ssm_conv_w': ssm_conv_w, 'ssm_conv_b': ssm_conv_b,
            'ssm_a_log': ssm_a_log, 'ssm_d': ssm_d, 'ssm_norm_w': ssm_norm_w,
            'w_proj_m': w_proj_m, 'w_proj_s': w_proj_s, 'w_out': w_out, 'final_w': final_w}


def reference(x, c, norm_w, ada_w, ada_b, w_in, b_in, ml_conv_w, ml_conv_b, ml_norm_w,
              ssm_conv_w, ssm_conv_b, ssm_a_log, ssm_d, ssm_norm_w, w_proj_m, w_proj_s,
              w_out, final_w):
    b, s, _ = x.shape
    for l in range(DEPTH):
        mod = jax.nn.silu(c) @ ada_w[l] + ada_b[l]
        shift, scale, gate = jnp.split(mod, 3, axis=-1)
        u = _rmsnorm(x, norm_w[l]) * (1.0 + scale[:, None, :]) + shift[:, None, :]
        proj = u @ w_in[l] + b_in[l]
        qk, v, o_pre, z_m, i_pre, f_pre, xbc, z_s, dt_raw, merge_pre = jnp.split(proj, SEG_SPLITS, axis=-1)

        qk = jax.nn.silu(_causal_dwconv(qk, ml_conv_w[l], ml_conv_b[l]))
        q, k = jnp.split(qk, 2, axis=-1)
        h_m = _mlstm(q.reshape(b, s, ML_HEADS, ML_DQK), k.reshape(b, s, ML_HEADS, ML_DQK),
                     v.reshape(b, s, ML_HEADS, ML_DV), i_pre, f_pre)
        h_m = _rmsnorm(h_m, ml_norm_w[l].reshape(ML_HEADS, ML_DV)).reshape(b, s, ML_INNER).astype(x.dtype)
        y_m = jax.nn.sigmoid(o_pre) * h_m * jax.nn.silu(z_m)

        xbc = jax.nn.silu(_causal_dwconv(xbc, ssm_conv_w[l], ssm_conv_b[l]))
        xs, bs, cs = jnp.split(xbc, (SSM_INNER, SSM_INNER + SSM_GROUPS * SSM_STATE), axis=-1)
        dt = jax.nn.softplus(dt_raw.astype(jnp.float32))
        a_coef = -jnp.exp(ssm_a_log[l].astype(jnp.float32))
        xh = xs.reshape(b, s, SSM_HEADS, SSM_HEADDIM)
        y_s = _ssd(xh, dt, a_coef, bs.reshape(b, s, SSM_GROUPS, SSM_STATE), cs.reshape(b, s, SSM_GROUPS, SSM_STATE))
        y_s = y_s + ssm_d[l][:, None] * xh
        y_s = (y_s.reshape(b, s, SSM_INNER) * jax.nn.silu(z_s)).reshape(b, s, SSM_GROUPS, SSM_INNER // SSM_GROUPS)
        y_s = _rmsnorm(y_s, ssm_norm_w[l].reshape(SSM_GROUPS, SSM_INNER // SSM_GROUPS)).reshape(b, s, SSM_INNER).astype(x.dtype)

        gate_m, gate_s = jnp.split(jax.nn.sigmoid(merge_pre), 2, axis=-1)
        merged = gate_m * (y_m @ w_proj_m[l]) + gate_s * (y_s @ w_proj_s[l])
        x = x + gate[:, None, :] * (merged @ w_out[l])
    return _rmsnorm(x, final_w)
```

```python
import functools

import jax
import jax.numpy as jnp
from jax import lax
from jax.experimental import pallas as pl
from jax.experimental.pallas import tpu as pltpu

F32 = jnp.float32
BF16 = jnp.bfloat16

D_MODEL = 1024
CHUNK = 64
CONV_K = 4
EPS = 1e-6
ML_HEADS = 8
ML_INNER = 2 * D_MODEL
ML_DV = ML_INNER // ML_HEADS
ML_DQK = ML_DV // 2
ML_QK2 = 2 * ML_HEADS * ML_DQK
SSM_INNER = 2 * D_MODEL
SSM_HEADDIM = 64
SSM_HEADS = SSM_INNER // SSM_HEADDIM
SSM_GROUPS = 4
SSM_STATE = 128
SSM_BC = 2 * SSM_GROUPS * SSM_STATE
SSM_XBC = SSM_INNER + SSM_BC
HEADS_PER_GROUP = SSM_HEADS // SSM_GROUPS

SEG_SIZES = (ML_QK2, ML_INNER, ML_INNER, ML_INNER, ML_HEADS, ML_HEADS,
             SSM_XBC, SSM_INNER, SSM_HEADS, 2 * D_MODEL)
SEG_OFF = tuple(sum(SEG_SIZES[:i]) for i in range(len(SEG_SIZES)))

SEG_W = 2048
BIG_QK, BIG_V, BIG_O, BIG_ZM, BIG_XS, BIG_ZS, BIG_MERGE = range(7)
BIG_BC_BLOCK = 7 * SEG_W // SSM_BC
BIG_WIDTH = 7 * SEG_W + SSM_BC

LANES = 128
GATE_I = 0
GATE_F = ML_HEADS
GATE_DT = 2 * ML_HEADS

BF16_SUBLANES = 16
VMEM_LIMIT = 48 * 1024 * 1024


def _silu(x):
    return x * jax.nn.sigmoid(x)


def _softplus(x):
    return jnp.maximum(x, 0.0) + jnp.log1p(jnp.exp(-jnp.abs(x)))


def _split3(x):
    hi = x.astype(BF16)
    r1 = x - hi.astype(F32)
    mid = r1.astype(BF16)
    lo = (r1 - mid.astype(F32)).astype(BF16)
    return hi, mid, lo


def _cumsum_rows(x, tri):
    hi, mid, lo = _split3(x)
    return (jnp.dot(tri, hi, preferred_element_type=F32)
            + jnp.dot(tri, mid, preferred_element_type=F32)
            + jnp.dot(tri, lo, preferred_element_type=F32))


def _lower_tri(n):
    r = lax.broadcasted_iota(jnp.int32, (n, n), 0)
    c = lax.broadcasted_iota(jnp.int32, (n, n), 1)
    return (c <= r).astype(BF16)


def _causal_conv_silu(x, tail, w_ref, b_ref):
    n = x.shape[0]
    xe = jnp.concatenate([tail, x], axis=0)
    y = b_ref[...]
    for j in range(CONV_K):
        s = CONV_K - 1 - j
        y = y + xe[8 - s:8 - s + n, :] * w_ref[j:j + 1, :]
    return _silu(y)


def _mod_kernel(c_ref, w_ref, b_ref, o_ref):
    c = c_ref[...]
    s = _silu(c)
    w = w_ref[...]
    s_hi = s.astype(BF16)
    s_lo = (s - s_hi.astype(F32)).astype(BF16)
    w_hi = w.astype(BF16)
    w_lo = (w - w_hi.astype(F32)).astype(BF16)
    acc = jnp.dot(s_hi, w_hi, preferred_element_type=F32)
    acc = acc + jnp.dot(s_hi, w_lo, preferred_element_type=F32)
    acc = acc + jnp.dot(s_lo, w_hi, preferred_element_type=F32)
    o_ref[...] = acc + b_ref[...]


def _mod(c_pad, ada_w, ada_b):
    rows = c_pad.shape[0]
    n = ada_w.shape[1]
    tn = D_MODEL
    return pl.pallas_call(
        _mod_kernel,
        grid=(n // tn,),
        in_specs=[pl.BlockSpec((rows, D_MODEL), lambda j: (0, 0)),
                  pl.BlockSpec((D_MODEL, tn), lambda j: (0, j)),
                  pl.BlockSpec((1, tn), lambda j: (0, j))],
        out_specs=pl.BlockSpec((rows, tn), lambda j: (0, j)),
        out_shape=jax.ShapeDtypeStruct((rows, n), F32),
        compiler_params=pltpu.CompilerParams(dimension_semantics=("arbitrary",),
                                             vmem_limit_bytes=VMEM_LIMIT),
        name="mod",
    )(c_pad, ada_w, ada_b)


def _inproj_kernel(x_ref, nw_ref, sc_ref, sh_ref, wb_ref, bb_ref, ws_ref, bs_ref,
                   ob_ref, os_ref, u_ref):
    j = pl.program_id(2)

    @pl.when(j == 0)
    def _():
        xf = x_ref[...]
        ms = jnp.mean(xf * xf, axis=-1, keepdims=True)
        y = xf * lax.rsqrt(ms + EPS) * nw_ref[...]
        u = y * (1.0 + sc_ref[...]) + sh_ref[...]
        u_hi = u.astype(BF16)
        u_ref[...] = u_hi
        u_lo = (u - u_hi.astype(F32)).astype(BF16)
        ws = ws_ref[...]
        ws_hi = ws.astype(BF16)
        ws_lo = (ws - ws_hi.astype(F32)).astype(BF16)
        acc = jnp.dot(u_hi, ws_hi, preferred_element_type=F32)
        acc = acc + jnp.dot(u_hi, ws_lo, preferred_element_type=F32)
        acc = acc + jnp.dot(u_lo, ws_hi, preferred_element_type=F32)
        os_ref[...] = acc + bs_ref[...]

    acc = jnp.dot(u_ref[...], wb_ref[...], preferred_element_type=F32)
    ob_ref[...] = (acc + bb_ref[...]).astype(BF16)


def _inproj(x, norm_w, scale, shift, w_big, b_big, w_small, b_small, *, tm, tn):
    b, s, _ = x.shape
    grid = (b, s // tm, BIG_WIDTH // tn)
    return pl.pallas_call(
        _inproj_kernel,
        grid=grid,
        in_specs=[
            pl.BlockSpec((None, tm, D_MODEL), lambda bi, i, j: (bi, i, 0)),
            pl.BlockSpec((1, D_MODEL), lambda bi, i, j: (0, 0)),
            pl.BlockSpec((None, 1, D_MODEL), lambda bi, i, j: (bi, 0, 0)),
            pl.BlockSpec((None, 1, D_MODEL), lambda bi, i, j: (bi, 0, 0)),
            pl.BlockSpec((D_MODEL, tn), lambda bi, i, j: (0, j)),
            pl.BlockSpec((1, tn), lambda bi, i, j: (0, j)),
            pl.BlockSpec((D_MODEL, LANES), lambda bi, i, j: (0, 0)),
            pl.BlockSpec((1, LANES), lambda bi, i, j: (0, 0)),
        ],
        out_specs=[
            pl.BlockSpec((None, tm, tn), lambda bi, i, j: (bi, i, j)),
            pl.BlockSpec((None, tm, LANES), lambda bi, i, j: (bi, i, 0)),
        ],
        out_shape=[jax.ShapeDtypeStruct((b, s, BIG_WIDTH), BF16),
                   jax.ShapeDtypeStruct((b, s, LANES), F32)],
        scratch_shapes=[pltpu.VMEM((tm, D_MODEL), BF16)],
        compiler_params=pltpu.CompilerParams(
            dimension_semantics=("parallel", "parallel", "arbitrary"),
            vmem_limit_bytes=VMEM_LIMIT),
        name="inproj",
    )(x, norm_w, scale, shift, w_big, b_big, w_small, b_small)


def _mlstm_kernel(qk_ref, qkt_ref, v_ref, g_ref, cw_ref, cb_ref, h_ref, c_scr, n_scr, m_scr):
    ci = pl.program_id(1)

    @pl.when(ci == 0)
    def _():
        c_scr[...] = jnp.zeros_like(c_scr)
        n_scr[...] = jnp.zeros_like(n_scr)
        m_scr[...] = jnp.zeros_like(m_scr)

    x = qk_ref[...].astype(F32)
    tail = qkt_ref[...].astype(F32)[BF16_SUBLANES - 8:, :]
    tail = jnp.where(ci > 0, tail, 0.0)
    qk = _causal_conv_silu(x, tail, cw_ref, cb_ref)

    g = g_ref[...]
    logf = jnp.minimum(g, 0.0) - jnp.log1p(jnp.exp(-jnp.abs(g)))
    tri = _lower_tri(CHUNK)
    a = _cumsum_rows(logf, tri)
    a_t = a.T
    g_t = g.T
    row = lax.broadcasted_iota(jnp.int32, (CHUNK, CHUNK), 0)
    col = lax.broadcasted_iota(jnp.int32, (CHUNK, CHUNK), 1)
    causal = col <= row

    for h in range(ML_HEADS):
        q = qk[:, h * ML_DQK:(h + 1) * ML_DQK] * (ML_DQK ** -0.5)
        k = qk[:, ML_HEADS * ML_DQK + h * ML_DQK:ML_HEADS * ML_DQK + (h + 1) * ML_DQK]
        v = v_ref[:, h * ML_DV:(h + 1) * ML_DV]
        q_bf = q.astype(BF16)
        lf = GATE_F + h
        a_col = a[:, lf:lf + 1]
        i_col = g[:, GATE_I + h:GATE_I + h + 1]
        r_row = g_t[GATE_I + h:GATE_I + h + 1, :] - a_t[lf:lf + 1, :]
        a_last = a[CHUNK - 1:CHUNK, lf:lf + 1]
        m_prev = m_scr[h:h + 1, 0:1]
        c_prev = c_scr[h]
        n_prev = n_scr[h:h + 1, :]

        dmat = jnp.where(causal, a_col + r_row, -jnp.inf)
        inter = a_col + m_prev
        m_t = jnp.maximum(inter, jnp.max(dmat, axis=1, keepdims=True))
        w_intra = jnp.exp(dmat - m_t)
        w_inter = jnp.exp(inter - m_t)
        sc = lax.dot_general(q_bf, k.astype(BF16), (((1,), (1,)), ((), ())),
                             preferred_element_type=F32) * w_intra
        num = (jnp.dot(sc.astype(BF16), v, preferred_element_type=F32)
               + w_inter * jnp.dot(q_bf, c_prev.astype(BF16), preferred_element_type=F32))
        den = (jnp.sum(sc, axis=1, keepdims=True)
               + w_inter * jnp.sum(q * n_prev, axis=1, keepdims=True))
        hb = num / jnp.maximum(jnp.abs(den), jnp.exp(-m_t))
        h_ref[:, h * ML_DV:(h + 1) * ML_DV] = hb.astype(h_ref.dtype)

        gk = a_last - a_col + i_col
        m_new = jnp.maximum(a_last + m_prev, jnp.max(gk, axis=0, keepdims=True))
        wk = jnp.exp(gk - m_new)
        decay = jnp.exp(a_last + m_prev - m_new)
        kw = k * wk
        c_scr[h] = decay * c_prev + jnp.dot(kw.T.astype(BF16), v, preferred_element_type=F32)
        n_scr[h:h + 1, :] = decay * n_prev + jnp.sum(kw, axis=0, keepdims=True)
        m_scr[h:h + 1, :] = jnp.broadcast_to(m_new, (1, LANES))


def _mlstm(proj_big, gates, conv_w, conv_b):
    b, s, _ = proj_big.shape
    nc = s // CHUNK
    tail_blocks = CHUNK // BF16_SUBLANES
    return pl.pallas_call(
        _mlstm_kernel,
        grid=(b, nc),
        in_specs=[
            pl.BlockSpec((None, CHUNK, SEG_W), lambda bi, c: (bi, c, BIG_QK)),
            pl.BlockSpec((None, BF16_SUBLANES, SEG_W),
                         lambda bi, c: (bi, jnp.maximum(c * tail_blocks - 1, 0), BIG_QK)),
            pl.BlockSpec((None, CHUNK, SEG_W), lambda bi, c: (bi, c, BIG_V)),
            pl.BlockSpec((None, CHUNK, LANES), lambda bi, c: (bi, c, 0)),
            pl.BlockSpec((CONV_K, ML_QK2), lambda bi, c: (0, 0)),
            pl.BlockSpec((1, ML_QK2), lambda bi, c: (0, 0)),
        ],
        out_specs=pl.BlockSpec((None, CHUNK, ML_INNER), lambda bi, c: (bi, c, 0)),
        out_shape=jax.ShapeDtypeStruct((b, s, ML_INNER), BF16),
        scratch_shapes=[pltpu.VMEM((ML_HEADS, ML_DQK, ML_DV), F32),
                        pltpu.VMEM((ML_HEADS, ML_DQK), F32),
                        pltpu.VMEM((ML_HEADS, LANES), F32)],
        compiler_params=pltpu.CompilerParams(dimension_semantics=("parallel", "arbitrary"),
                                             vmem_limit_bytes=VMEM_LIMIT),
        name="mlstm",
    )(proj_big, proj_big, proj_big, gates, conv_w, conv_b)


def _ssd_kernel(xs_ref, xst_ref, bc_ref, bct_ref, g_ref, cwx_ref, cbx_ref, cwb_ref, cbb_ref,
                acoef_ref, dskip_ref, y_ref, st_scr):
    ci = pl.program_id(1)

    @pl.when(ci == 0)
    def _():
        st_scr[...] = jnp.zeros_like(st_scr)

    keep = ci > 0
    xt = jnp.where(keep, xst_ref[...].astype(F32)[BF16_SUBLANES - 8:, :], 0.0)
    bt = jnp.where(keep, bct_ref[...].astype(F32)[BF16_SUBLANES - 8:, :], 0.0)
    xs = _causal_conv_silu(xs_ref[...].astype(F32), xt, cwx_ref, cbx_ref)
    bc = _causal_conv_silu(bc_ref[...].astype(F32), bt, cwb_ref, cbb_ref)
    ngs = SSM_GROUPS * SSM_STATE

    g = g_ref[...]
    dt = _softplus(g)
    tri = _lower_tri(CHUNK)
    a = _cumsum_rows(dt * acoef_ref[...], tri)
    a2_t = jnp.concatenate([a, a], axis=0).T
    dt2_t = jnp.concatenate([dt, dt], axis=0).T
    left = lax.broadcasted_iota(jnp.int32, (1, 2 * CHUNK), 1) < CHUNK
    row = lax.broadcasted_iota(jnp.int32, (CHUNK, 2 * CHUNK), 0)
    col = lax.broadcasted_iota(jnp.int32, (CHUNK, 2 * CHUNK), 1)
    causal2 = jnp.bitwise_and(col, CHUNK - 1) <= row

    for gi in range(SSM_GROUPS):
        bg = bc[:, gi * SSM_STATE:(gi + 1) * SSM_STATE]
        cg = bc[:, ngs + gi * SSM_STATE:ngs + (gi + 1) * SSM_STATE].astype(BF16)
        bg_bf = bg.astype(BF16)
        b2 = jnp.concatenate([bg_bf, bg_bf], axis=0)
        scores2 = lax.dot_general(cg, b2, (((1,), (1,)), ((), ())),
                                  preferred_element_type=F32)
        st = st_scr[gi]
        y_inter = jnp.dot(cg, st.astype(BF16), preferred_element_type=F32)
        bg_t = bg.T.astype(BF16)
        for p in range(HEADS_PER_GROUP // 2):
            ha = gi * HEADS_PER_GROUP + 2 * p
            la = GATE_DT + ha
            lb = la + 1
            slab = slice((ha // 2) * 2 * SSM_HEADDIM, (ha // 2 + 1) * 2 * SSM_HEADDIM)
            sl_g = slice(p * 2 * SSM_HEADDIM, (p + 1) * 2 * SSM_HEADDIM)
            a_col2 = jnp.where(left, a[:, la:la + 1], a[:, lb:lb + 1])
            dt_col2 = jnp.where(left, dt[:, la:la + 1], dt[:, lb:lb + 1])
            a_row2 = jnp.where(left, a2_t[la:la + 1, :], a2_t[lb:lb + 1, :])
            dt_row2 = jnp.where(left, dt2_t[la:la + 1, :], dt2_t[lb:lb + 1, :])
            a_last2 = jnp.where(left, a[CHUNK - 1:CHUNK, la:la + 1], a[CHUNK - 1:CHUNK, lb:lb + 1])

            decay2 = jnp.exp(jnp.where(causal2, a_col2 - a_row2, -jnp.inf))
            mix2 = (decay2 * scores2 * dt_row2).astype(BF16)
            xp = xs[:, slab]
            xp_bd = jnp.concatenate([jnp.where(left, xp, 0.0), jnp.where(left, 0.0, xp)],
                                    axis=0).astype(BF16)
            y = (jnp.dot(mix2, xp_bd, preferred_element_type=F32)
                 + jnp.exp(a_col2) * y_inter[:, sl_g]
                 + dskip_ref[:, slab] * xp)
            y_ref[:, slab] = y.astype(y_ref.dtype)

            wts2 = jnp.exp(a_last2 - a_col2) * dt_col2
            xw = (xp * wts2).astype(BF16)
            st_scr[gi, :, sl_g] = (jnp.exp(a_last2) * st[:, sl_g]
                                   + jnp.dot(bg_t, xw, preferred_element_type=F32))


def _ssd(proj_big, gates, cw_x, cb_x, cw_bc, cb_bc, acoef_row, dskip_row):
    b, s, _ = proj_big.shape
    nc = s // CHUNK
    tail_blocks = CHUNK // BF16_SUBLANES

    def tail_idx(blk):
        return lambda bi, c: (bi, jnp.maximum(c * tail_blocks - 1, 0), blk)

    return pl.pallas_call(
        _ssd_kernel,
        grid=(b, nc),
        in_specs=[
            pl.BlockSpec((None, CHUNK, SEG_W), lambda bi, c: (bi, c, BIG_XS)),
            pl.BlockSpec((None, BF16_SUBLANES, SEG_W), tail_idx(BIG_XS)),
            pl.BlockSpec((None, CHUNK, SSM_BC), lambda bi, c: (bi, c, BIG_BC_BLOCK)),
            pl.BlockSpec((None, BF16_SUBLANES, SSM_BC), tail_idx(BIG_BC_BLOCK)),
            pl.BlockSpec((None, CHUNK, LANES), lambda bi, c: (bi, c, 0)),
            pl.BlockSpec((CONV_K, SSM_INNER), lambda bi, c: (0, 0)),
            pl.BlockSpec((1, SSM_INNER), lambda bi, c: (0, 0)),
            pl.BlockSpec((CONV_K, SSM_BC), lambda bi, c: (0, 0)),
            pl.BlockSpec((1, SSM_BC), lambda bi, c: (0, 0)),
            pl.BlockSpec((1, LANES), lambda bi, c: (0, 0)),
            pl.BlockSpec((1, SSM_INNER), lambda bi, c: (0, 0)),
        ],
        out_specs=pl.BlockSpec((None, CHUNK, SSM_INNER), lambda bi, c: (bi, c, 0)),
        out_shape=jax.ShapeDtypeStruct((b, s, SSM_INNER), BF16),
        scratch_shapes=[pltpu.VMEM((SSM_GROUPS, SSM_STATE, HEADS_PER_GROUP * SSM_HEADDIM), F32)],
        compiler_params=pltpu.CompilerParams(dimension_semantics=("parallel", "arbitrary"),
                                             vmem_limit_bytes=VMEM_LIMIT),
        name="ssd",
    )(proj_big, proj_big, proj_big, proj_big, gates, cw_x, cb_x, cw_bc, cb_bc,
      acoef_row, dskip_row)


def _outproj_kernel(h_ref, o_ref, zm_ref, ys_ref, zs_ref, mg_ref, x_ref, gate_ref,
                    mnw_ref, snw_ref, wm_ref, ws_ref, wo_ref, fw_ref, out_ref, ym_scr, yn_scr):
    for hd in range(ML_HEADS):
        sl = slice(hd * ML_DV, (hd + 1) * ML_DV)
        hh = h_ref[:, sl].astype(F32)
        ms = jnp.mean(hh * hh, axis=-1, keepdims=True)
        hn = hh * lax.rsqrt(ms + EPS) * mnw_ref[:, sl]
        ym = jax.nn.sigmoid(o_ref[:, sl].astype(F32)) * hn * _silu(zm_ref[:, sl].astype(F32))
        ym_scr[:, sl] = ym.astype(BF16)
    gw = SSM_INNER // SSM_GROUPS
    for gi in range(SSM_GROUPS):
        sl = slice(gi * gw, (gi + 1) * gw)
        t = ys_ref[:, sl].astype(F32) * _silu(zs_ref[:, sl].astype(F32))
        ms = jnp.mean(t * t, axis=-1, keepdims=True)
        yn_scr[:, sl] = (t * lax.rsqrt(ms + EPS) * snw_ref[:, sl]).astype(BF16)
    pm = jnp.dot(ym_scr[...], wm_ref[...], preferred_element_type=F32)
    ps = jnp.dot(yn_scr[...], ws_ref[...], preferred_element_type=F32)
    gm = jax.nn.sigmoid(mg_ref[:, :D_MODEL].astype(F32))
    gs = jax.nn.sigmoid(mg_ref[:, D_MODEL:].astype(F32))
    merged = (gm * pm + gs * ps).astype(BF16)
    r = jnp.dot(merged, wo_ref[...], preferred_element_type=F32)
    xo = x_ref[...] + gate_ref[...] * r
    ms = jnp.mean(xo * xo, axis=-1, keepdims=True)
    out_ref[...] = xo * lax.rsqrt(ms + EPS) * fw_ref[...]


def _outproj(h_m, y_s, proj_big, x, gate, ml_norm_w, ssm_norm_w, w_m, w_s, w_o, final_w, *, tm):
    b, s, _ = x.shape

    def seg(k):
        return pl.BlockSpec((None, tm, SEG_W), lambda bi, i: (bi, i, k))

    def full(shape):
        return pl.BlockSpec(shape, lambda bi, i: (0,) * len(shape))

    return pl.pallas_call(
        _outproj_kernel,
        grid=(b, s // tm),
        in_specs=[
            pl.BlockSpec((None, tm, ML_INNER), lambda bi, i: (bi, i, 0)),
            seg(BIG_O), seg(BIG_ZM),
            pl.BlockSpec((None, tm, SSM_INNER), lambda bi, i: (bi, i, 0)),
            seg(BIG_ZS), seg(BIG_MERGE),
            pl.BlockSpec((None, tm, D_MODEL), lambda bi, i: (bi, i, 0)),
            pl.BlockSpec((None, 1, D_MODEL), lambda bi, i: (bi, 0, 0)),
            full((1, ML_INNER)), full((1, SSM_INNER)),
            full((ML_INNER, D_MODEL)), full((SSM_INNER, D_MODEL)), full((D_MODEL, D_MODEL)),
            full((1, D_MODEL)),
        ],
        out_specs=pl.BlockSpec((None, tm, D_MODEL), lambda bi, i: (bi, i, 0)),
        out_shape=jax.ShapeDtypeStruct((b, s, D_MODEL), F32),
        scratch_shapes=[pltpu.VMEM((tm, ML_INNER), BF16), pltpu.VMEM((tm, SSM_INNER), BF16)],
        compiler_params=pltpu.CompilerParams(dimension_semantics=("parallel", "parallel"),
                                             vmem_limit_bytes=VMEM_LIMIT),
        name="outproj",
    )(h_m, proj_big, proj_big, y_s, proj_big, proj_big, x, gate, ml_norm_w, ssm_norm_w,
      w_m, w_s, w_o, final_w)


def _seg(w, k):
    return w[..., SEG_OFF[k]:SEG_OFF[k] + SEG_SIZES[k]]


def _layer(x, c_pad, norm_w, ada_w, ada_b, w_in, b_in, ml_conv_w, ml_conv_b, ml_norm_w,
           ssm_conv_w, ssm_conv_b, ssm_a_log, ssm_d, ssm_norm_w, w_proj_m, w_proj_s, w_out,
           final_w):
    b, s, _ = x.shape
    mod = _mod(c_pad, ada_w, ada_b.reshape(1, -1))[:b]
    shift = mod[:, :D_MODEL].reshape(b, 1, D_MODEL)
    scale = mod[:, D_MODEL:2 * D_MODEL].reshape(b, 1, D_MODEL)
    gate = mod[:, 2 * D_MODEL:].reshape(b, 1, D_MODEL)

    xbc_w, xbc_b = _seg(w_in, 6), _seg(b_in, 6)
    w_big = jnp.concatenate(
        [_seg(w_in, 0), _seg(w_in, 1), _seg(w_in, 2), _seg(w_in, 3), xbc_w[:, :SSM_INNER],
         _seg(w_in, 7), _seg(w_in, 9), xbc_w[:, SSM_INNER:]], axis=1).astype(BF16)
    b_big = jnp.concatenate(
        [_seg(b_in, 0), _seg(b_in, 1), _seg(b_in, 2), _seg(b_in, 3), xbc_b[:SSM_INNER],
         _seg(b_in, 7), _seg(b_in, 9), xbc_b[SSM_INNER:]]).reshape(1, BIG_WIDTH)
    pad = LANES - (2 * ML_HEADS + SSM_HEADS)
    w_small = jnp.pad(jnp.concatenate([_seg(w_in, 4), _seg(w_in, 5), _seg(w_in, 8)], axis=1),
                      ((0, 0), (0, pad)))
    b_small = jnp.pad(jnp.concatenate([_seg(b_in, 4), _seg(b_in, 5), _seg(b_in, 8)]),
                      (0, pad)).reshape(1, LANES)

    proj_big, gates = _inproj(x, norm_w.reshape(1, D_MODEL), scale, shift, w_big, b_big,
                              w_small, b_small, tm=min(1024, s), tn=1536)

    h_m = _mlstm(proj_big, gates, ml_conv_w, ml_conv_b.reshape(1, -1))

    acoef_row = jnp.pad(-jnp.exp(ssm_a_log.astype(F32)), (GATE_DT, pad)).reshape(1, LANES)
    dskip_row = jnp.repeat(ssm_d, SSM_HEADDIM).reshape(1, SSM_INNER)
    y_s = _ssd(proj_big, gates, ssm_conv_w[:, :SSM_INNER], ssm_conv_b[:SSM_INNER].reshape(1, -1),
               ssm_conv_w[:, SSM_INNER:], ssm_conv_b[SSM_INNER:].reshape(1, -1),
               acoef_row, dskip_row)

    return _outproj(h_m, y_s, proj_big, x, gate, ml_norm_w.reshape(1, -1),
                    ssm_norm_w.reshape(1, -1), w_proj_m.astype(BF16), w_proj_s.astype(BF16),
                    w_out.astype(BF16), final_w.reshape(1, D_MODEL), tm=256)


def kernel(x, c, norm_w, ada_w, ada_b, w_in, b_in, ml_conv_w, ml_conv_b, ml_norm_w,
           ssm_conv_w, ssm_conv_b, ssm_a_log, ssm_d, ssm_norm_w, w_proj_m, w_proj_s, w_out,
           final_w):
    b = x.shape[0]
    depth = norm_w.shape[0]
    assert depth == 1, "the fused residual + final-norm epilogue assumes a single layer"
    c_pad = jnp.pad(c, ((0, (-b) % 8), (0, 0)))
    return _layer(x, c_pad, norm_w[0], ada_w[0], ada_b[0], w_in[0], b_in[0], ml_conv_w[0],
                  ml_conv_b[0], ml_norm_w[0], ssm_conv_w[0], ssm_conv_b[0], ssm_a_log[0],
                  ssm_d[0], ssm_norm_w[0], w_proj_m[0], w_proj_s[0], w_out[0], final_w)
```

```python
import jax
import jax.numpy as jnp
from jax import lax
from jax.experimental import pallas as pl
from jax.experimental.pallas import tpu as pltpu

F32 = jnp.float32
BF16 = jnp.bfloat16

D_MODEL = 1024
CHUNK = 64
ML_CHUNK = 128
CONV_K = 4
EPS = 1e-6
ML_HEADS = 8
ML_INNER = 2 * D_MODEL
ML_DV = ML_INNER // ML_HEADS
ML_DQK = ML_DV // 2
ML_QK2 = 2 * ML_HEADS * ML_DQK
SSM_INNER = 2 * D_MODEL
SSM_HEADDIM = 64
SSM_HEADS = SSM_INNER // SSM_HEADDIM
SSM_GROUPS = 4
SSM_STATE = 128
SSM_BC = 2 * SSM_GROUPS * SSM_STATE
SSM_XBC = SSM_INNER + SSM_BC
HEADS_PER_GROUP = SSM_HEADS // SSM_GROUPS

SEG_SIZES = (ML_QK2, ML_INNER, ML_INNER, ML_INNER, ML_HEADS, ML_HEADS,
             SSM_XBC, SSM_INNER, SSM_HEADS, 2 * D_MODEL)
SEG_OFF = tuple(sum(SEG_SIZES[:i]) for i in range(len(SEG_SIZES)))

SEG_W = 2048
CONV_QK, CONV_XS = 0, 1
CONV_BC_BLOCK = 2 * SEG_W // SSM_BC
CONV_WIDTH = 2 * SEG_W + SSM_BC
PLAIN_V, PLAIN_O, PLAIN_ZM, PLAIN_ZS, PLAIN_MERGE = range(5)
PLAIN_WIDTH = 5 * SEG_W

LANES = 128
GATE_W = 2 * LANES
GATE_DT = 16

SUBLANES = 8
VMEM_LIMIT = 48 * 1024 * 1024


def _sigmoid(x):
    return 0.5 * jnp.tanh(0.5 * x) + 0.5


def _silu(x):
    return x * _sigmoid(x)


def _softplus(x):
    return jnp.maximum(x, 0.0) + jnp.log1p(jnp.exp(-jnp.abs(x)))


def _log_sigmoid(x):
    return jnp.minimum(x, 0.0) - jnp.log1p(jnp.exp(-jnp.abs(x)))


def _split2(x):
    hi = x.astype(BF16)
    lo = (x - hi.astype(F32)).astype(BF16)
    return hi, lo


def _split3(x):
    hi = x.astype(BF16)
    r1 = x - hi.astype(F32)
    mid = r1.astype(BF16)
    lo = (r1 - mid.astype(F32)).astype(BF16)
    return hi, mid, lo


def _dot3(a, b):
    a_hi, a_lo = _split2(a)
    b_hi, b_lo = _split2(b)
    acc = jnp.dot(a_hi, b_hi, preferred_element_type=F32)
    acc = acc + jnp.dot(a_hi, b_lo, preferred_element_type=F32)
    return acc + jnp.dot(a_lo, b_hi, preferred_element_type=F32)


def _cumsum_rows(x, tri):
    hi, mid, lo = _split3(x)
    return (jnp.dot(tri, hi, preferred_element_type=F32)
            + jnp.dot(tri, mid, preferred_element_type=F32)
            + jnp.dot(tri, lo, preferred_element_type=F32))


def _shift_rows(x, d, fill):
    n, w = x.shape
    xe = jnp.concatenate([jnp.full((SUBLANES, w), fill, x.dtype), x], axis=0)
    return xe[SUBLANES - d:SUBLANES - d + n, :]


def _cummax_rows(x):
    n, w = x.shape
    d = 1
    while d < n:
        if d < SUBLANES:
            sh = _shift_rows(x, d, -jnp.inf)
        else:
            sh = jnp.concatenate([jnp.full((d, w), -jnp.inf, x.dtype), x[:n - d, :]], axis=0)
        x = jnp.maximum(x, sh)
        d *= 2
    return x


def _lower_tri(n):
    r = lax.broadcasted_iota(jnp.int32, (n, n), 0)
    c = lax.broadcasted_iota(jnp.int32, (n, n), 1)
    return (c <= r).astype(BF16)


def _adaln(x_ref, nw_ref, sc_ref, sh_ref):
    xf = x_ref[...]
    ms = jnp.mean(xf * xf, axis=-1, keepdims=True)
    y = xf * lax.rsqrt(ms + EPS) * nw_ref[...]
    return y * (1.0 + sc_ref[...]) + sh_ref[...]


def _mod_kernel(c_ref, w_ref, b_ref, o_ref):
    o_ref[...] = _dot3(_silu(c_ref[...]), w_ref[...]) + b_ref[...]


def _mod(c_pad, ada_w, ada_b):
    rows = c_pad.shape[0]
    n = ada_w.shape[1]
    tn = D_MODEL
    return pl.pallas_call(
        _mod_kernel,
        grid=(n // tn,),
        in_specs=[pl.BlockSpec((rows, D_MODEL), lambda j: (0, 0)),
                  pl.BlockSpec((D_MODEL, tn), lambda j: (0, j)),
                  pl.BlockSpec((1, tn), lambda j: (0, j))],
        out_specs=pl.BlockSpec((rows, tn), lambda j: (0, j)),
        out_shape=jax.ShapeDtypeStruct((rows, n), F32),
        compiler_params=pltpu.CompilerParams(dimension_semantics=("arbitrary",),
                                             vmem_limit_bytes=VMEM_LIMIT),
        name="mod",
    )(c_pad, ada_w, ada_b)


def _inproj_conv_kernel(x_ref, nw_ref, sc_ref, sh_ref, w_ref, b_ref, cw_ref, cb_ref, ps_ref,
                        wg_ref, bg_ref, o_ref, og_ref, u_ref, tail_ref):
    i = pl.program_id(1)
    j = pl.program_id(2)
    tm = x_ref.shape[0]

    @pl.when(j == 0)
    def _():
        u = _adaln(x_ref, nw_ref, sc_ref, sh_ref)
        u_ref[...] = u.astype(BF16)
        og_ref[...] = _dot3(u, wg_ref[...]) + bg_ref[...]

    @pl.when(i == 0)
    def _():
        tail_ref[j] = jnp.zeros(tail_ref.shape[1:], F32)

    acc = jnp.dot(u_ref[...], w_ref[...], preferred_element_type=F32) + b_ref[...]
    tail = tail_ref[j]
    tail_ref[j] = acc[tm - SUBLANES:, :]
    xe = jnp.concatenate([tail, acc], axis=0)
    y = cb_ref[...]
    for k in range(CONV_K):
        s = CONV_K - 1 - k
        y = y + xe[SUBLANES - s:SUBLANES - s + tm, :] * cw_ref[k:k + 1, :]
    o_ref[...] = (_silu(y) * ps_ref[...]).astype(o_ref.dtype)


def _inproj_conv(x, norm_w, scale, shift, w, b, conv_w, conv_b, post_scale, w_gate, b_gate,
                 *, tm, tn):
    bsz, s, _ = x.shape
    ncol = CONV_WIDTH // tn
    row = lambda bi, i, j: (bi, i, 0)
    col = lambda bi, i, j: (0, j)
    fix = lambda bi, i, j: (0, 0)
    per_b = lambda bi, i, j: (bi, 0, 0)
    return pl.pallas_call(
        _inproj_conv_kernel,
        grid=(bsz, s // tm, ncol),
        in_specs=[
            pl.BlockSpec((None, tm, D_MODEL), row),
            pl.BlockSpec((1, D_MODEL), fix),
            pl.BlockSpec((None, 1, D_MODEL), per_b),
            pl.BlockSpec((None, 1, D_MODEL), per_b),
            pl.BlockSpec((D_MODEL, tn), col),
            pl.BlockSpec((1, tn), col),
            pl.BlockSpec((CONV_K, tn), col),
            pl.BlockSpec((1, tn), col),
            pl.BlockSpec((1, tn), col),
            pl.BlockSpec((D_MODEL, GATE_W), fix),
            pl.BlockSpec((1, GATE_W), fix),
        ],
        out_specs=[
            pl.BlockSpec((None, tm, tn), lambda bi, i, j: (bi, i, j)),
            pl.BlockSpec((None, tm, GATE_W), row),
        ],
        out_shape=[jax.ShapeDtypeStruct((bsz, s, CONV_WIDTH), BF16),
                   jax.ShapeDtypeStruct((bsz, s, GATE_W), F32)],
        scratch_shapes=[pltpu.VMEM((tm, D_MODEL), BF16),
                        pltpu.VMEM((ncol, SUBLANES, tn), F32)],
        compiler_params=pltpu.CompilerParams(
            dimension_semantics=("parallel", "arbitrary", "arbitrary"),
            vmem_limit_bytes=VMEM_LIMIT),
        name="inproj_conv",
    )(x, norm_w, scale, shift, w, b, conv_w, conv_b, post_scale, w_gate, b_gate)


def _inproj_kernel(x_ref, nw_ref, sc_ref, sh_ref, w_ref, b_ref, o_ref, u_ref):
    @pl.when(pl.program_id(2) == 0)
    def _():
        u_ref[...] = _adaln(x_ref, nw_ref, sc_ref, sh_ref).astype(BF16)

    acc = jnp.dot(u_ref[...], w_ref[...], preferred_element_type=F32)
    o_ref[...] = (acc + b_ref[...]).astype(o_ref.dtype)


def _inproj(x, norm_w, scale, shift, w, b, *, tm, tn):
    bsz, s, _ = x.shape
    return pl.pallas_call(
        _inproj_kernel,
        grid=(bsz, s // tm, PLAIN_WIDTH // tn),
        in_specs=[
            pl.BlockSpec((None, tm, D_MODEL), lambda bi, i, j: (bi, i, 0)),
            pl.BlockSpec((1, D_MODEL), lambda bi, i, j: (0, 0)),
            pl.BlockSpec((None, 1, D_MODEL), lambda bi, i, j: (bi, 0, 0)),
            pl.BlockSpec((None, 1, D_MODEL), lambda bi, i, j: (bi, 0, 0)),
            pl.BlockSpec((D_MODEL, tn), lambda bi, i, j: (0, j)),
            pl.BlockSpec((1, tn), lambda bi, i, j: (0, j)),
        ],
        out_specs=pl.BlockSpec((None, tm, tn), lambda bi, i, j: (bi, i, j)),
        out_shape=jax.ShapeDtypeStruct((bsz, s, PLAIN_WIDTH), BF16),
        scratch_shapes=[pltpu.VMEM((tm, D_MODEL), BF16)],
        compiler_params=pltpu.CompilerParams(
            dimension_semantics=("parallel", "parallel", "arbitrary"),
            vmem_limit_bytes=VMEM_LIMIT),
        name="inproj",
    )(x, norm_w, scale, shift, w, b)


def _mlstm_kernel(qk_ref, v_ref, g_ref, h_ref, c_scr, n_scr, m_scr):
    @pl.when(pl.program_id(1) == 0)
    def _():
        c_scr[...] = jnp.zeros_like(c_scr)
        n_scr[...] = jnp.zeros_like(n_scr)
        m_scr[...] = jnp.zeros_like(m_scr)

    n = ML_CHUNK
    tri = _lower_tri(n)
    row = lax.broadcasted_iota(jnp.int32, (n, n), 0)
    col = lax.broadcasted_iota(jnp.int32, (n, n), 1)
    causal = col <= row

    for ch in range(qk_ref.shape[0] // n):
        rows = slice(ch * n, (ch + 1) * n)
        a = _cumsum_rows(_log_sigmoid(g_ref[rows, LANES:]), tri)
        r = g_ref[rows, :LANES] - a
        m_prev = m_scr[...]
        mm = jnp.maximum(_cummax_rows(r), m_prev)
        w_inter = jnp.exp(m_prev - mm)
        floor = jnp.exp(-(a + mm))
        mm_last = mm[n - 1:n, :]
        wk_all = jnp.exp(r - mm_last)
        m_scr[...] = a[n - 1:n, :] + mm_last
        r_t = r.T

        for h in range(ML_HEADS):
            q = qk_ref[rows, h * ML_DQK:(h + 1) * ML_DQK]
            k = qk_ref[rows, ML_HEADS * ML_DQK + h * ML_DQK:ML_HEADS * ML_DQK + (h + 1) * ML_DQK]
            v = v_ref[rows, h * ML_DV:(h + 1) * ML_DV]
            c_prev = c_scr[h]
            n_prev = n_scr[h:h + 1, :]
            w_in = w_inter[:, h:h + 1]

            w_intra = jnp.exp(jnp.where(causal, r_t[h:h + 1, :] - mm[:, h:h + 1], -jnp.inf))
            sc = lax.dot_general(q, k, (((1,), (1,)), ((), ())),
                                 preferred_element_type=F32) * w_intra
            num = (jnp.dot(sc.astype(BF16), v, preferred_element_type=F32)
                   + w_in * jnp.dot(q, c_prev.astype(BF16), preferred_element_type=F32))
            den = (jnp.sum(sc, axis=1, keepdims=True)
                   + w_in * jnp.sum(q.astype(F32) * n_prev, axis=1, keepdims=True))
            inv = 1.0 / jnp.maximum(jnp.abs(den), floor[:, h:h + 1])
            h_ref[rows, h * ML_DV:(h + 1) * ML_DV] = (num * inv).astype(h_ref.dtype)

            decay = w_inter[n - 1:n, h:h + 1]
            kw = k.astype(F32) * wk_all[:, h:h + 1]
            c_scr[h] = decay * c_prev + jnp.dot(kw.T.astype(BF16), v, preferred_element_type=F32)
            n_scr[h:h + 1, :] = decay * n_prev + jnp.sum(kw, axis=0, keepdims=True)


def _mlstm(conv_out, plain_out, gates, *, rows):
    b, s, _ = conv_out.shape
    return pl.pallas_call(
        _mlstm_kernel,
        grid=(b, s // rows),
        in_specs=[
            pl.BlockSpec((None, rows, SEG_W), lambda bi, c: (bi, c, CONV_QK)),
            pl.BlockSpec((None, rows, SEG_W), lambda bi, c: (bi, c, PLAIN_V)),
            pl.BlockSpec((None, rows, GATE_W), lambda bi, c: (bi, c, 0)),
        ],
        out_specs=pl.BlockSpec((None, rows, ML_INNER), lambda bi, c: (bi, c, 0)),
        out_shape=jax.ShapeDtypeStruct((b, s, ML_INNER), BF16),
        scratch_shapes=[pltpu.VMEM((ML_HEADS, ML_DQK, ML_DV), F32),
                        pltpu.VMEM((ML_HEADS, ML_DQK), F32),
                        pltpu.VMEM((1, LANES), F32)],
        compiler_params=pltpu.CompilerParams(dimension_semantics=("parallel", "arbitrary"),
                                             vmem_limit_bytes=VMEM_LIMIT),
        name="mlstm",
    )(conv_out, plain_out, gates)


def _ssd_kernel(xs_ref, bc_ref, g_ref, acoef_ref, dskip_ref, y_ref, st_scr):
    @pl.when(pl.program_id(1) == 0)
    def _():
        st_scr[...] = jnp.zeros_like(st_scr)

    ngs = SSM_GROUPS * SSM_STATE
    pair_w = 2 * SSM_HEADDIM
    tri = _lower_tri(CHUNK)
    lane = lax.broadcasted_iota(jnp.int32, (1, pair_w), 1)
    left = lane < SSM_HEADDIM
    left_bf = left.astype(BF16)
    right_bf = (lane >= SSM_HEADDIM).astype(BF16)
    row = lax.broadcasted_iota(jnp.int32, (CHUNK, 2 * CHUNK), 0)
    col = lax.broadcasted_iota(jnp.int32, (CHUNK, 2 * CHUNK), 1)
    causal2 = jnp.bitwise_and(col, CHUNK - 1) <= row

    for ch in range(xs_ref.shape[0] // CHUNK):
        rows = slice(ch * CHUNK, (ch + 1) * CHUNK)
        dt = _softplus(g_ref[rows, :LANES])
        a = _cumsum_rows(dt * acoef_ref[...], tri)
        a2_t = jnp.concatenate([a, a], axis=0).T
        dt2_t = jnp.concatenate([dt, dt], axis=0).T

        for gi in range(SSM_GROUPS):
            bg = bc_ref[rows, gi * SSM_STATE:(gi + 1) * SSM_STATE]
            cg = bc_ref[rows, ngs + gi * SSM_STATE:ngs + (gi + 1) * SSM_STATE]
            b2 = jnp.concatenate([bg, bg], axis=0)
            scores2 = lax.dot_general(cg, b2, (((1,), (1,)), ((), ())),
                                      preferred_element_type=F32)
            st = st_scr[gi]
            y_inter = jnp.dot(cg, st.astype(BF16), preferred_element_type=F32)
            bg_t = bg.astype(F32).T.astype(BF16)
            for p in range(HEADS_PER_GROUP // 2):
                pair = gi * (HEADS_PER_GROUP // 2) + p
                la = GATE_DT + 2 * pair
                lb = la + 1
                slab = slice(pair * pair_w, (pair + 1) * pair_w)
                sl_g = slice(p * pair_w, (p + 1) * pair_w)
                a_col2 = jnp.where(left, a[:, la:la + 1], a[:, lb:lb + 1])
                dt_col2 = jnp.where(left, dt[:, la:la + 1], dt[:, lb:lb + 1])
                a_row2 = jnp.where(left, a2_t[la:la + 1, :], a2_t[lb:lb + 1, :])
                dt_row2 = jnp.where(left, dt2_t[la:la + 1, :], dt2_t[lb:lb + 1, :])
                a_last2 = a_col2[CHUNK - 1:CHUNK, :]

                decay2 = jnp.exp(jnp.where(causal2, a_col2 - a_row2, -jnp.inf))
                mix2 = (decay2 * scores2 * dt_row2).astype(BF16)
                xp_bf = xs_ref[rows, slab]
                xp = xp_bf.astype(F32)
                xp_bd = jnp.concatenate([xp_bf * left_bf, xp_bf * right_bf], axis=0)
                y = (jnp.dot(mix2, xp_bd, preferred_element_type=F32)
                     + jnp.exp(a_col2) * y_inter[:, sl_g]
                     + dskip_ref[:, slab] * xp)
                y_ref[rows, slab] = y.astype(y_ref.dtype)

                wts2 = jnp.exp(a_last2 - a_col2) * dt_col2
                xw = (xp * wts2).astype(BF16)
                st_scr[gi, :, sl_g] = (jnp.exp(a_last2) * st[:, sl_g]
                                       + jnp.dot(bg_t, xw, preferred_element_type=F32))


def _ssd(conv_out, gates, acoef_row, dskip_row, *, rows):
    b, s, _ = conv_out.shape
    return pl.pallas_call(
        _ssd_kernel,
        grid=(b, s // rows),
        in_specs=[
            pl.BlockSpec((None, rows, SEG_W), lambda bi, c: (bi, c, CONV_XS)),
            pl.BlockSpec((None, rows, SSM_BC), lambda bi, c: (bi, c, CONV_BC_BLOCK)),
            pl.BlockSpec((None, rows, LANES), lambda bi, c: (bi, c, 0)),
            pl.BlockSpec((1, LANES), lambda bi, c: (0, 0)),
            pl.BlockSpec((1, SSM_INNER), lambda bi, c: (0, 0)),
        ],
        out_specs=pl.BlockSpec((None, rows, SSM_INNER), lambda bi, c: (bi, c, 0)),
        out_shape=jax.ShapeDtypeStruct((b, s, SSM_INNER), BF16),
        scratch_shapes=[pltpu.VMEM((SSM_GROUPS, SSM_STATE, HEADS_PER_GROUP * SSM_HEADDIM), F32)],
        compiler_params=pltpu.CompilerParams(dimension_semantics=("parallel", "arbitrary"),
                                             vmem_limit_bytes=VMEM_LIMIT),
        name="ssd",
    )(conv_out, conv_out, gates, acoef_row, dskip_row)


def _outproj_kernel(h_ref, o_ref, zm_ref, ys_ref, zs_ref, mg_ref, x_ref, gate_ref,
                    mnw_ref, snw_ref, wm_ref, ws_ref, wo_ref, fw_ref, out_ref, ym_scr, yn_scr):
    for hd in range(ML_HEADS):
        sl = slice(hd * ML_DV, (hd + 1) * ML_DV)
        hh = h_ref[:, sl].astype(F32)
        ms = jnp.mean(hh * hh, axis=-1, keepdims=True)
        hn = hh * lax.rsqrt(ms + EPS) * mnw_ref[:, sl]
        ym = _sigmoid(o_ref[:, sl].astype(F32)) * hn * _silu(zm_ref[:, sl].astype(F32))
        ym_scr[:, sl] = ym.astype(BF16)
    gw = SSM_INNER // SSM_GROUPS
    for gi in range(SSM_GROUPS):
        sl = slice(gi * gw, (gi + 1) * gw)
        t = ys_ref[:, sl].astype(F32) * _silu(zs_ref[:, sl].astype(F32))
        ms = jnp.mean(t * t, axis=-1, keepdims=True)
        yn_scr[:, sl] = (t * lax.rsqrt(ms + EPS) * snw_ref[:, sl]).astype(BF16)
    pm = jnp.dot(ym_scr[...], wm_ref[...], preferred_element_type=F32)
    ps = jnp.dot(yn_scr[...], ws_ref[...], preferred_element_type=F32)
    gm = _sigmoid(mg_ref[:, :D_MODEL].astype(F32))
    gs = _sigmoid(mg_ref[:, D_MODEL:].astype(F32))
    merged = (gm * pm + gs * ps).astype(BF16)
    r = jnp.dot(merged, wo_ref[...], preferred_element_type=F32)
    xo = x_ref[...] + gate_ref[...] * r
    ms = jnp.mean(xo * xo, axis=-1, keepdims=True)
    out_ref[...] = xo * lax.rsqrt(ms + EPS) * fw_ref[...]


def _outproj(h_m, y_s, plain_out, x, gate, ml_norm_w, ssm_norm_w, w_m, w_s, w_o, final_w, *, tm):
    b, s, _ = x.shape

    def seg(k):
        return pl.BlockSpec((None, tm, SEG_W), lambda bi, i: (bi, i, k))

    def full(shape):
        return pl.BlockSpec(shape, lambda bi, i: (0,) * len(shape))

    return pl.pallas_call(
        _outproj_kernel,
        grid=(b, s // tm),
        in_specs=[
            pl.BlockSpec((None, tm, ML_INNER), lambda bi, i: (bi, i, 0)),
            seg(PLAIN_O), seg(PLAIN_ZM),
            pl.BlockSpec((None, tm, SSM_INNER), lambda bi, i: (bi, i, 0)),
            seg(PLAIN_ZS), seg(PLAIN_MERGE),
            pl.BlockSpec((None, tm, D_MODEL), lambda bi, i: (bi, i, 0)),
            pl.BlockSpec((None, 1, D_MODEL), lambda bi, i: (bi, 0, 0)),
            full((1, ML_INNER)), full((1, SSM_INNER)),
            full((ML_INNER, D_MODEL)), full((SSM_INNER, D_MODEL)), full((D_MODEL, D_MODEL)),
            full((1, D_MODEL)),
        ],
        out_specs=pl.BlockSpec((None, tm, D_MODEL), lambda bi, i: (bi, i, 0)),
        out_shape=jax.ShapeDtypeStruct((b, s, D_MODEL), F32),
        scratch_shapes=[pltpu.VMEM((tm, ML_INNER), BF16), pltpu.VMEM((tm, SSM_INNER), BF16)],
        compiler_params=pltpu.CompilerParams(dimension_semantics=("parallel", "parallel"),
                                             vmem_limit_bytes=VMEM_LIMIT),
        name="outproj",
    )(h_m, plain_out, plain_out, y_s, plain_out, plain_out, x, gate, ml_norm_w, ssm_norm_w,
      w_m, w_s, w_o, final_w)


def _seg(w, k):
    return w[..., SEG_OFF[k]:SEG_OFF[k] + SEG_SIZES[k]]


def _gate_cols(i_cols, f_cols, dt_cols):
    lead = i_cols.shape[:-1]

    def z(n):
        return jnp.zeros(lead + (n,), i_cols.dtype)

    return jnp.concatenate(
        [i_cols, z(GATE_DT - ML_HEADS), dt_cols, z(LANES - GATE_DT - SSM_HEADS),
         f_cols, z(LANES - ML_HEADS)], axis=-1)


def _layer(x, c_pad, norm_w, ada_w, ada_b, w_in, b_in, ml_conv_w, ml_conv_b, ml_norm_w,
           ssm_conv_w, ssm_conv_b, ssm_a_log, ssm_d, ssm_norm_w, w_proj_m, w_proj_s, w_out,
           final_w):
    b, s, _ = x.shape
    mod = _mod(c_pad, ada_w, ada_b.reshape(1, -1))[:b]
    shift = mod[:, :D_MODEL].reshape(b, 1, D_MODEL)
    scale = mod[:, D_MODEL:2 * D_MODEL].reshape(b, 1, D_MODEL)
    gate = mod[:, 2 * D_MODEL:].reshape(b, 1, D_MODEL)
    norm_w = norm_w.reshape(1, D_MODEL)

    w_conv = jnp.concatenate([_seg(w_in, 0), _seg(w_in, 6)], axis=1).astype(BF16)
    b_conv = jnp.concatenate([_seg(b_in, 0), _seg(b_in, 6)]).reshape(1, CONV_WIDTH)
    conv_w = jnp.concatenate([ml_conv_w, ssm_conv_w], axis=1)
    conv_b = jnp.concatenate([ml_conv_b, ssm_conv_b]).reshape(1, CONV_WIDTH)
    n_q = ML_HEADS * ML_DQK
    post_scale = jnp.concatenate([jnp.full((n_q,), ML_DQK ** -0.5, F32),
                                  jnp.ones((CONV_WIDTH - n_q,), F32)]).reshape(1, CONV_WIDTH)
    w_plain = jnp.concatenate([_seg(w_in, 1), _seg(w_in, 2), _seg(w_in, 3), _seg(w_in, 7),
                               _seg(w_in, 9)], axis=1).astype(BF16)
    b_plain = jnp.concatenate([_seg(b_in, 1), _seg(b_in, 2), _seg(b_in, 3), _seg(b_in, 7),
                               _seg(b_in, 9)]).reshape(1, PLAIN_WIDTH)
    w_gate = _gate_cols(_seg(w_in, 4), _seg(w_in, 5), _seg(w_in, 8))
    b_gate = _gate_cols(_seg(b_in, 4), _seg(b_in, 5), _seg(b_in, 8)).reshape(1, GATE_W)

    conv_out, gates = _inproj_conv(x, norm_w, scale, shift, w_conv, b_conv, conv_w, conv_b,
                                   post_scale, w_gate, b_gate, tm=min(512, s), tn=1280)
    plain_out = _inproj(x, norm_w, scale, shift, w_plain, b_plain, tm=min(1024, s), tn=2048)

    scan_rows = min(4 * CHUNK, s)
    h_m = _mlstm(conv_out, plain_out, gates, rows=min(2 * ML_CHUNK, s))

    acoef_row = jnp.pad(-jnp.exp(ssm_a_log.astype(F32)),
                        (GATE_DT, LANES - GATE_DT - SSM_HEADS)).reshape(1, LANES)
    dskip_row = jnp.repeat(ssm_d, SSM_HEADDIM).reshape(1, SSM_INNER)
    y_s = _ssd(conv_out, gates, acoef_row, dskip_row, rows=scan_rows)

    return _outproj(h_m, y_s, plain_out, x, gate, ml_norm_w.reshape(1, -1),
                    ssm_norm_w.reshape(1, -1), w_proj_m.astype(BF16), w_proj_s.astype(BF16),
                    w_out.astype(BF16), final_w.reshape(1, D_MODEL), tm=256)


def kernel(x, c, norm_w, ada_w, ada_b, w_in, b_in, ml_conv_w, ml_conv_b, ml_norm_w,
           ssm_conv_w, ssm_conv_b, ssm_a_log, ssm_d, ssm_norm_w, w_proj_m, w_proj_s, w_out,
           final_w):
    b = x.shape[0]
    assert norm_w.shape[0] == 1, "the residual + final-norm epilogue is fused for a single layer"
    c_pad = jnp.pad(c, ((0, (-b) % SUBLANES), (0, 0)))
    return _layer(x, c_pad, norm_w[0], ada_w[0], ada_b[0], w_in[0], b_in[0], ml_conv_w[0],
                  ml_conv_b[0], ml_norm_w[0], ssm_conv_w[0], ssm_conv_b[0], ssm_a_log[0],
                  ssm_d[0], ssm_norm_w[0], w_proj_m[0], w_proj_s[0], w_out[0], final_w)
```

```python
import jax
import jax.numpy as jnp
from jax import lax
from jax.experimental import pallas as pl
from jax.experimental.pallas import tpu as pltpu

F32 = jnp.float32
BF16 = jnp.bfloat16

D_MODEL = 1024
CHUNK = 64
ML_CHUNK = 128
CONV_K = 4
EPS = 1e-6
ML_HEADS = 8
ML_INNER = 2 * D_MODEL
ML_DV = ML_INNER // ML_HEADS
ML_DQK = ML_DV // 2
ML_QK2 = 2 * ML_HEADS * ML_DQK
SSM_INNER = 2 * D_MODEL
SSM_HEADDIM = 64
SSM_HEADS = SSM_INNER // SSM_HEADDIM
SSM_GROUPS = 4
SSM_STATE = 128
SSM_BC = 2 * SSM_GROUPS * SSM_STATE
SSM_XBC = SSM_INNER + SSM_BC
HEADS_PER_GROUP = SSM_HEADS // SSM_GROUPS

SEG_SIZES = (ML_QK2, ML_INNER, ML_INNER, ML_INNER, ML_HEADS, ML_HEADS,
             SSM_XBC, SSM_INNER, SSM_HEADS, 2 * D_MODEL)
SEG_OFF = tuple(sum(SEG_SIZES[:i]) for i in range(len(SEG_SIZES)))

SEG_W = 2048
CONV_QK, CONV_XS = 0, 1
CONV_BC_BLOCK = 2 * SEG_W // SSM_BC
CONV_WIDTH = 2 * SEG_W + SSM_BC
PLAIN_V, PLAIN_O, PLAIN_ZM, PLAIN_ZS, PLAIN_MERGE = range(5)
PLAIN_WIDTH = 5 * SEG_W

LANES = 128
GATE_W = 2 * LANES
GATE_DT = 16

SUBLANES = 8
MXU_WIDTH = 256
INPROJ_PIECE = MXU_WIDTH
VMEM_LIMIT = 48 * 1024 * 1024


def _sigmoid(x):
    return 0.5 * jnp.tanh(0.5 * x) + 0.5


def _silu(x):
    return x * _sigmoid(x)


def _softplus(x):
    return jnp.maximum(x, 0.0) + jnp.log1p(jnp.exp(-jnp.abs(x)))


def _log_sigmoid(x):
    return jnp.minimum(x, 0.0) - jnp.log1p(jnp.exp(-jnp.abs(x)))


def _split2(x):
    hi = x.astype(BF16)
    lo = (x - hi.astype(F32)).astype(BF16)
    return hi, lo


def _split3(x):
    hi = x.astype(BF16)
    r1 = x - hi.astype(F32)
    mid = r1.astype(BF16)
    lo = (r1 - mid.astype(F32)).astype(BF16)
    return hi, mid, lo


def _dot3(a, b):
    a_hi, a_lo = _split2(a)
    b_hi, b_lo = _split2(b)
    acc = jnp.dot(a_hi, b_hi, preferred_element_type=F32)
    acc = acc + jnp.dot(a_hi, b_lo, preferred_element_type=F32)
    return acc + jnp.dot(a_lo, b_hi, preferred_element_type=F32)


def _cumsum_rows(x, tri):
    hi, mid, lo = _split3(x)
    return (jnp.dot(tri, hi, preferred_element_type=F32)
            + jnp.dot(tri, mid, preferred_element_type=F32)
            + jnp.dot(tri, lo, preferred_element_type=F32))


def _shift_rows(x, d, fill):
    n, w = x.shape
    xe = jnp.concatenate([jnp.full((SUBLANES, w), fill, x.dtype), x], axis=0)
    return xe[SUBLANES - d:SUBLANES - d + n, :]


def _cummax_rows(x):
    n, w = x.shape
    d = 1
    while d < n:
        if d < SUBLANES:
            sh = _shift_rows(x, d, -jnp.inf)
        else:
            sh = jnp.concatenate([jnp.full((d, w), -jnp.inf, x.dtype), x[:n - d, :]], axis=0)
        x = jnp.maximum(x, sh)
        d *= 2
    return x


def _lower_tri(n):
    r = lax.broadcasted_iota(jnp.int32, (n, n), 0)
    c = lax.broadcasted_iota(jnp.int32, (n, n), 1)
    return (c <= r).astype(BF16)


def _adaln(x_ref, nw_ref, sc_ref, sh_ref):
    xf = x_ref[...]
    ms = jnp.mean(xf * xf, axis=-1, keepdims=True)
    y = xf * lax.rsqrt(ms + EPS) * nw_ref[...]
    return y * (1.0 + sc_ref[...]) + sh_ref[...]


def _mod_kernel(c_ref, w_ref, b_ref, o_ref):
    o_ref[...] = _dot3(_silu(c_ref[...]), w_ref[...]) + b_ref[...]


def _mod(c_pad, ada_w, ada_b):
    rows = c_pad.shape[0]
    n = ada_w.shape[1]
    tn = D_MODEL
    return pl.pallas_call(
        _mod_kernel,
        grid=(n // tn,),
        in_specs=[pl.BlockSpec((rows, D_MODEL), lambda j: (0, 0)),
                  pl.BlockSpec((D_MODEL, tn), lambda j: (0, j)),
                  pl.BlockSpec((1, tn), lambda j: (0, j))],
        out_specs=pl.BlockSpec((rows, tn), lambda j: (0, j)),
        out_shape=jax.ShapeDtypeStruct((rows, n), F32),
        compiler_params=pltpu.CompilerParams(dimension_semantics=("arbitrary",),
                                             vmem_limit_bytes=VMEM_LIMIT),
        name="mod",
    )(c_pad, ada_w, ada_b)


def _shift_down(xe, s):
    n = xe.shape[0] - SUBLANES
    w = xe.shape[1]
    rot = pltpu.roll(xe.reshape(-1, SUBLANES, w), s, axis=1).reshape(xe.shape)
    sub = lax.broadcasted_iota(jnp.int32, (n, w), 0) % SUBLANES
    return jnp.where(sub < s, rot[:n, :], rot[SUBLANES:, :])


def _inproj_kernel(x_ref, nw_ref, sc_ref, sh_ref, wc_ref, bc_ref, cw_ref, cb_ref, post_ref,
                   wp_ref, bp_ref, wg_ref, bg_ref, oc_ref, op_ref, og_ref, u_ref, tail_ref):
    i = pl.program_id(1)
    j = pl.program_id(2)
    tm = x_ref.shape[0]

    @pl.when(j == 0)
    def _():
        u = _adaln(x_ref, nw_ref, sc_ref, sh_ref)
        u_ref[...] = u.astype(BF16)
        og_ref[...] = _dot3(u, wg_ref[...]) + bg_ref[...]

    @pl.when(i == 0)
    def _():
        tail_ref[j] = jnp.broadcast_to(-bc_ref[...], tail_ref.shape[1:])

    pieces = wc_ref.shape[1] // INPROJ_PIECE
    pw = wp_ref.shape[1] // pieces
    for p in range(pieces):
        cs = slice(p * INPROJ_PIECE, (p + 1) * INPROJ_PIECE)
        ps = slice(p * pw, (p + 1) * pw)
        u = u_ref[...]
        acc = jnp.dot(u, wc_ref[:, cs], preferred_element_type=F32)
        op_ref[:, ps] = (jnp.dot(u, wp_ref[:, ps], preferred_element_type=F32)
                         + bp_ref[:, ps]).astype(op_ref.dtype)

        xe = jnp.concatenate([tail_ref[j, :, cs], acc], axis=0)
        tail_ref[j, :, cs] = acc[tm - SUBLANES:, :]
        yh = cb_ref[:, cs] + acc * cw_ref[CONV_K - 1:CONV_K, cs]
        for k in range(CONV_K - 1):
            yh = yh + _shift_down(xe, CONV_K - 1 - k) * cw_ref[k:k + 1, cs]
        post = post_ref[:, cs]
        oc_ref[:, cs] = (yh * (post * jnp.tanh(yh) + post)).astype(oc_ref.dtype)


def _inproj(x, norm_w, scale, shift, w_conv, b_conv, conv_w, conv_b, post_scale, w_plain, b_plain,
            w_gate, b_gate, *, tm, ncol):
    bsz, s, _ = x.shape
    tc = CONV_WIDTH // ncol
    tp = PLAIN_WIDTH // ncol
    row = lambda bi, i, j: (bi, i, 0)
    col = lambda bi, i, j: (0, j)
    fix = lambda bi, i, j: (0, 0)
    per_b = lambda bi, i, j: (bi, 0, 0)
    tile = lambda bi, i, j: (bi, i, j)
    return pl.pallas_call(
        _inproj_kernel,
        grid=(bsz, s // tm, ncol),
        in_specs=[
            pl.BlockSpec((None, tm, D_MODEL), row),
            pl.BlockSpec((1, D_MODEL), fix),
            pl.BlockSpec((None, 1, D_MODEL), per_b),
            pl.BlockSpec((None, 1, D_MODEL), per_b),
            pl.BlockSpec((D_MODEL, tc), col),
            pl.BlockSpec((1, tc), col),
            pl.BlockSpec((CONV_K, tc), col),
            pl.BlockSpec((1, tc), col),
            pl.BlockSpec((1, tc), col),
            pl.BlockSpec((D_MODEL, tp), col),
            pl.BlockSpec((1, tp), col),
            pl.BlockSpec((D_MODEL, GATE_W), fix),
            pl.BlockSpec((1, GATE_W), fix),
        ],
        out_specs=[
            pl.BlockSpec((None, tm, tc), tile),
            pl.BlockSpec((None, tm, tp), tile),
            pl.BlockSpec((None, tm, GATE_W), row),
        ],
        out_shape=[jax.ShapeDtypeStruct((bsz, s, CONV_WIDTH), BF16),
                   jax.ShapeDtypeStruct((bsz, s, PLAIN_WIDTH), BF16),
                   jax.ShapeDtypeStruct((bsz, s, GATE_W), F32)],
        scratch_shapes=[pltpu.VMEM((tm, D_MODEL), BF16),
                        pltpu.VMEM((ncol, SUBLANES, tc), F32)],
        compiler_params=pltpu.CompilerParams(
            dimension_semantics=("parallel", "arbitrary", "arbitrary"),
            vmem_limit_bytes=VMEM_LIMIT),
        name="inproj",
    )(x, norm_w, scale, shift, w_conv, b_conv, conv_w, conv_b, post_scale, w_plain, b_plain,
      w_gate, b_gate)


def _mlstm_kernel(qk_ref, v_ref, g_ref, h_ref, c_scr, n_scr, m_scr):
    @pl.when(pl.program_id(1) == 0)
    def _():
        c_scr[...] = jnp.zeros_like(c_scr)
        n_scr[...] = jnp.zeros_like(n_scr)
        m_scr[...] = jnp.zeros_like(m_scr)

    n = ML_CHUNK
    tri = _lower_tri(n)
    row = lax.broadcasted_iota(jnp.int32, (n, n), 0)
    col = lax.broadcasted_iota(jnp.int32, (n, n), 1)
    causal = col <= row

    for ch in range(qk_ref.shape[0] // n):
        rows = slice(ch * n, (ch + 1) * n)
        a = _cumsum_rows(_log_sigmoid(g_ref[rows, LANES:]), tri)
        r = g_ref[rows, :LANES] - a
        m_prev = m_scr[...]
        mm = jnp.maximum(_cummax_rows(r), m_prev)
        w_inter = jnp.exp(m_prev - mm)
        floor = jnp.exp(-(a + mm))
        mm_last = mm[n - 1:n, :]
        wk_all = jnp.exp(r - mm_last)
        m_scr[...] = a[n - 1:n, :] + mm_last
        r_t = r.T

        for h in range(ML_HEADS):
            q = qk_ref[rows, h * ML_DQK:(h + 1) * ML_DQK]
            k = qk_ref[rows, ML_HEADS * ML_DQK + h * ML_DQK:ML_HEADS * ML_DQK + (h + 1) * ML_DQK]
            v = v_ref[rows, h * ML_DV:(h + 1) * ML_DV]
            c_prev = c_scr[h]
            n_prev = n_scr[h:h + 1, :]
            w_in = w_inter[:, h:h + 1]

            w_intra = jnp.exp(jnp.where(causal, r_t[h:h + 1, :] - mm[:, h:h + 1], -jnp.inf))
            sc = lax.dot_general(q, k, (((1,), (1,)), ((), ())),
                                 preferred_element_type=F32) * w_intra
            num = (jnp.dot(sc.astype(BF16), v, preferred_element_type=F32)
                   + w_in * jnp.dot(q, c_prev.astype(BF16), preferred_element_type=F32))
            den = (jnp.sum(sc, axis=1, keepdims=True)
                   + w_in * jnp.sum(q.astype(F32) * n_prev, axis=1, keepdims=True))
            inv = 1.0 / jnp.maximum(jnp.abs(den), floor[:, h:h + 1])
            h_ref[rows, h * ML_DV:(h + 1) * ML_DV] = (num * inv).astype(h_ref.dtype)

            decay = w_inter[n - 1:n, h:h + 1]
            kw = k.astype(F32) * wk_all[:, h:h + 1]
            c_scr[h] = decay * c_prev + jnp.dot(kw.T.astype(BF16), v, preferred_element_type=F32)
            n_scr[h:h + 1, :] = decay * n_prev + jnp.sum(kw, axis=0, keepdims=True)


def _mlstm(conv_out, plain_out, gates, *, rows):
    b, s, _ = conv_out.shape
    return pl.pallas_call(
        _mlstm_kernel,
        grid=(b, s // rows),
        in_specs=[
            pl.BlockSpec((None, rows, SEG_W), lambda bi, c: (bi, c, CONV_QK)),
            pl.BlockSpec((None, rows, SEG_W), lambda bi, c: (bi, c, PLAIN_V)),
            pl.BlockSpec((None, rows, GATE_W), lambda bi, c: (bi, c, 0)),
        ],
        out_specs=pl.BlockSpec((None, rows, ML_INNER), lambda bi, c: (bi, c, 0)),
        out_shape=jax.ShapeDtypeStruct((b, s, ML_INNER), BF16),
        scratch_shapes=[pltpu.VMEM((ML_HEADS, ML_DQK, ML_DV), F32),
                        pltpu.VMEM((ML_HEADS, ML_DQK), F32),
                        pltpu.VMEM((1, LANES), F32)],
        compiler_params=pltpu.CompilerParams(dimension_semantics=("parallel", "arbitrary"),
                                             vmem_limit_bytes=VMEM_LIMIT),
        name="mlstm",
    )(conv_out, plain_out, gates)


def _ssd_kernel(xs_ref, bc_ref, g_ref, acoef_ref, dskip_ref, y_ref, st_scr):
    @pl.when(pl.program_id(1) == 0)
    def _():
        st_scr[...] = jnp.zeros_like(st_scr)

    ngs = SSM_GROUPS * SSM_STATE
    pair_w = 2 * SSM_HEADDIM
    tri = _lower_tri(CHUNK)
    lane = lax.broadcasted_iota(jnp.int32, (1, pair_w), 1)
    left = lane < SSM_HEADDIM
    left_bf = left.astype(BF16)
    right_bf = (lane >= SSM_HEADDIM).astype(BF16)
    row = lax.broadcasted_iota(jnp.int32, (CHUNK, 2 * CHUNK), 0)
    col = lax.broadcasted_iota(jnp.int32, (CHUNK, 2 * CHUNK), 1)
    causal2 = jnp.bitwise_and(col, CHUNK - 1) <= row

    for ch in range(xs_ref.shape[0] // CHUNK):
        rows = slice(ch * CHUNK, (ch + 1) * CHUNK)
        dt = _softplus(g_ref[rows, :LANES])
        a = _cumsum_rows(dt * acoef_ref[...], tri)
        a2_t = jnp.concatenate([a, a], axis=0).T
        dt2_t = jnp.concatenate([dt, dt], axis=0).T

        for gi in range(SSM_GROUPS):
            bg = bc_ref[rows, gi * SSM_STATE:(gi + 1) * SSM_STATE]
            cg = bc_ref[rows, ngs + gi * SSM_STATE:ngs + (gi + 1) * SSM_STATE]
            b2 = jnp.concatenate([bg, bg], axis=0)
            scores2 = lax.dot_general(cg, b2, (((1,), (1,)), ((), ())),
                                      preferred_element_type=F32)
            st = st_scr[gi]
            y_inter = jnp.dot(cg, st.astype(BF16), preferred_element_type=F32)
            bg_t = bg.astype(F32).T.astype(BF16)
            for p in range(HEADS_PER_GROUP // 2):
                pair = gi * (HEADS_PER_GROUP // 2) + p
                la = GATE_DT + 2 * pair
                lb = la + 1
                slab = slice(pair * pair_w, (pair + 1) * pair_w)
                sl_g = slice(p * pair_w, (p + 1) * pair_w)
                a_col2 = jnp.where(left, a[:, la:la + 1], a[:, lb:lb + 1])
                dt_col2 = jnp.where(left, dt[:, la:la + 1], dt[:, lb:lb + 1])
                a_row2 = jnp.where(left, a2_t[la:la + 1, :], a2_t[lb:lb + 1, :])
                dt_row2 = jnp.where(left, dt2_t[la:la + 1, :], dt2_t[lb:lb + 1, :])
                a_last2 = a_col2[CHUNK - 1:CHUNK, :]

                decay2 = jnp.exp(jnp.where(causal2, a_col2 - a_row2, -jnp.inf))
                mix2 = (decay2 * scores2 * dt_row2).astype(BF16)
                xp_bf = xs_ref[rows, slab]
                xp = xp_bf.astype(F32)
                xp_bd = jnp.concatenate([xp_bf * left_bf, xp_bf * right_bf], axis=0)
                y = (jnp.dot(mix2, xp_bd, preferred_element_type=F32)
                     + jnp.exp(a_col2) * y_inter[:, sl_g]
                     + dskip_ref[:, slab] * xp)
                y_ref[rows, slab] = y.astype(y_ref.dtype)

                wts2 = jnp.exp(a_last2 - a_col2) * dt_col2
                xw = (xp * wts2).astype(BF16)
                st_scr[gi, :, sl_g] = (jnp.exp(a_last2) * st[:, sl_g]
                                       + jnp.dot(bg_t, xw, preferred_element_type=F32))


def _ssd(conv_out, gates, acoef_row, dskip_row, *, rows):
    b, s, _ = conv_out.shape
    return pl.pallas_call(
        _ssd_kernel,
        grid=(b, s // rows),
        in_specs=[
            pl.BlockSpec((None, rows, SEG_W), lambda bi, c: (bi, c, CONV_XS)),
            pl.BlockSpec((None, rows, SSM_BC), lambda bi, c: (bi, c, CONV_BC_BLOCK)),
            pl.BlockSpec((None, rows, LANES), lambda bi, c: (bi, c, 0)),
            pl.BlockSpec((1, LANES), lambda bi, c: (0, 0)),
            pl.BlockSpec((1, SSM_INNER), lambda bi, c: (0, 0)),
        ],
        out_specs=pl.BlockSpec((None, rows, SSM_INNER), lambda bi, c: (bi, c, 0)),
        out_shape=jax.ShapeDtypeStruct((b, s, SSM_INNER), BF16),
        scratch_shapes=[pltpu.VMEM((SSM_GROUPS, SSM_STATE, HEADS_PER_GROUP * SSM_HEADDIM), F32)],
        compiler_params=pltpu.CompilerParams(dimension_semantics=("parallel", "arbitrary"),
                                             vmem_limit_bytes=VMEM_LIMIT),
        name="ssd",
    )(conv_out, conv_out, gates, acoef_row, dskip_row)


def _outproj_kernel(h_ref, o_ref, zm_ref, ys_ref, zs_ref, mg_ref, x_ref, gate_ref,
                    wm_ref, ws_ref, wo_ref, fw_ref, out_ref):
    ym = []
    for hd in range(ML_HEADS):
        sl = slice(hd * ML_DV, (hd + 1) * ML_DV)
        hh = h_ref[:, sl].astype(F32)
        rs = lax.rsqrt(jnp.mean(hh * hh, axis=-1, keepdims=True) + EPS)
        zh = zm_ref[:, sl].astype(F32)
        gates = (jnp.tanh(o_ref[:, sl].astype(F32)) + 1.0) * (jnp.tanh(zh) + 1.0)
        ym.append((gates * (zh * hh) * rs).astype(BF16))
    pm = jnp.dot(jnp.concatenate(ym, axis=1), wm_ref[...], preferred_element_type=F32)
    yn = []
    gw = SSM_INNER // SSM_GROUPS
    for gi in range(SSM_GROUPS):
        sl = slice(gi * gw, (gi + 1) * gw)
        zh = zs_ref[:, sl].astype(F32)
        t = ys_ref[:, sl].astype(F32) * zh * (jnp.tanh(zh) + 1.0)
        rs = lax.rsqrt(jnp.mean(t * t, axis=-1, keepdims=True) + EPS)
        yn.append((t * rs).astype(BF16))
    ps = jnp.dot(jnp.concatenate(yn, axis=1), ws_ref[...], preferred_element_type=F32)
    gm = jnp.tanh(mg_ref[:, :D_MODEL].astype(F32)) + 1.0
    gs = jnp.tanh(mg_ref[:, D_MODEL:].astype(F32)) + 1.0
    merged = (gm * pm + gs * ps).astype(BF16)
    r = jnp.dot(merged, wo_ref[...], preferred_element_type=F32)
    xo = x_ref[...] + gate_ref[...] * r
    ms = jnp.mean(xo * xo, axis=-1, keepdims=True)
    out_ref[...] = xo * lax.rsqrt(ms + EPS) * fw_ref[...]


def _outproj(h_m, y_s, plain_out, x, gate, w_m, w_s, w_o, final_w, *, tm):
    b, s, _ = x.shape

    def seg(k):
        return pl.BlockSpec((None, tm, SEG_W), lambda bi, i: (bi, i, k))

    def full(shape):
        return pl.BlockSpec(shape, lambda bi, i: (0,) * len(shape), pipeline_mode=pl.Buffered(1))

    return pl.pallas_call(
        _outproj_kernel,
        grid=(b, s // tm),
        in_specs=[
            pl.BlockSpec((None, tm, ML_INNER), lambda bi, i: (bi, i, 0)),
            seg(PLAIN_O), seg(PLAIN_ZM),
            pl.BlockSpec((None, tm, SSM_INNER), lambda bi, i: (bi, i, 0)),
            seg(PLAIN_ZS), seg(PLAIN_MERGE),
            pl.BlockSpec((None, tm, D_MODEL), lambda bi, i: (bi, i, 0)),
            pl.BlockSpec((None, 1, D_MODEL), lambda bi, i: (bi, 0, 0)),
            full((ML_INNER, D_MODEL)), full((SSM_INNER, D_MODEL)), full((D_MODEL, D_MODEL)),
            full((1, D_MODEL)),
        ],
        out_specs=pl.BlockSpec((None, tm, D_MODEL), lambda bi, i: (bi, i, 0)),
        out_shape=jax.ShapeDtypeStruct((b, s, D_MODEL), F32),
        compiler_params=pltpu.CompilerParams(dimension_semantics=("parallel", "parallel"),
                                             vmem_limit_bytes=VMEM_LIMIT),
        name="outproj",
    )(h_m, plain_out, plain_out, y_s, plain_out, plain_out, x, gate, w_m, w_s, w_o, final_w)


def _seg(w, k):
    return w[..., SEG_OFF[k]:SEG_OFF[k] + SEG_SIZES[k]]


def _gate_cols(i_cols, f_cols, dt_cols):
    lead = i_cols.shape[:-1]

    def z(n):
        return jnp.zeros(lead + (n,), i_cols.dtype)

    return jnp.concatenate(
        [i_cols, z(GATE_DT - ML_HEADS), dt_cols, z(LANES - GATE_DT - SSM_HEADS),
         f_cols, z(LANES - ML_HEADS)], axis=-1)


def _layer(x, c_pad, norm_w, ada_w, ada_b, w_in, b_in, ml_conv_w, ml_conv_b, ml_norm_w,
           ssm_conv_w, ssm_conv_b, ssm_a_log, ssm_d, ssm_norm_w, w_proj_m, w_proj_s, w_out,
           final_w):
    b, s, _ = x.shape
    mod = _mod(c_pad, ada_w, ada_b.reshape(1, -1))[:b]
    shift = mod[:, :D_MODEL].reshape(b, 1, D_MODEL)
    scale = mod[:, D_MODEL:2 * D_MODEL].reshape(b, 1, D_MODEL)
    gate = mod[:, 2 * D_MODEL:].reshape(b, 1, D_MODEL)
    norm_w = norm_w.reshape(1, D_MODEL)

    w_conv = jnp.concatenate([_seg(w_in, 0), _seg(w_in, 6)], axis=1).astype(BF16)
    b_conv = jnp.concatenate([_seg(b_in, 0), _seg(b_in, 6)]).reshape(1, CONV_WIDTH)
    conv_w = jnp.concatenate([ml_conv_w, ssm_conv_w], axis=1)
    conv_b = jnp.concatenate([ml_conv_b, ssm_conv_b]).reshape(1, CONV_WIDTH)
    conv_b_half = 0.5 * (conv_b + b_conv * jnp.sum(conv_w, axis=0, keepdims=True))
    conv_w_half = 0.5 * conv_w
    n_q = ML_HEADS * ML_DQK
    post_scale = jnp.concatenate([jnp.full((n_q,), ML_DQK ** -0.5, F32),
                                  jnp.ones((CONV_WIDTH - n_q,), F32)]).reshape(1, CONV_WIDTH)
    w_plain = jnp.concatenate([_seg(w_in, 1), 0.5 * _seg(w_in, 2), 0.5 * _seg(w_in, 3),
                               0.5 * _seg(w_in, 7), 0.5 * _seg(w_in, 9)], axis=1).astype(BF16)
    b_plain = jnp.concatenate([_seg(b_in, 1), 0.5 * _seg(b_in, 2), 0.5 * _seg(b_in, 3),
                               0.5 * _seg(b_in, 7), 0.5 * _seg(b_in, 9)]).reshape(1, PLAIN_WIDTH)
    w_gate = _gate_cols(_seg(w_in, 4), _seg(w_in, 5), _seg(w_in, 8))
    b_gate = _gate_cols(_seg(b_in, 4), _seg(b_in, 5), _seg(b_in, 8)).reshape(1, GATE_W)

    conv_out, plain_out, gates = _inproj(
        x, norm_w, scale, shift, w_conv, b_conv, conv_w_half, conv_b_half, post_scale,
        w_plain, b_plain, w_gate, b_gate, tm=min(512, s), ncol=4)

    scan_rows = min(4 * CHUNK, s)
    h_m = _mlstm(conv_out, plain_out, gates, rows=min(2 * ML_CHUNK, s))

    acoef_row = jnp.pad(-jnp.exp(ssm_a_log.astype(F32)),
                        (GATE_DT, LANES - GATE_DT - SSM_HEADS)).reshape(1, LANES)
    dskip_row = jnp.repeat(ssm_d, SSM_HEADDIM).reshape(1, SSM_INNER)
    y_s = _ssd(conv_out, gates, acoef_row, dskip_row, rows=scan_rows)

    w_m = ((0.5 * ml_norm_w)[:, None] * w_proj_m).astype(BF16)
    w_s = (ssm_norm_w[:, None] * w_proj_s).astype(BF16)
    w_o = (0.5 * w_out).astype(BF16)
    return _outproj(h_m, y_s, plain_out, x, gate, w_m, w_s, w_o, final_w.reshape(1, D_MODEL),
                    tm=min(512, s))


def kernel(x, c, norm_w, ada_w, ada_b, w_in, b_in, ml_conv_w, ml_conv_b, ml_norm_w,
           ssm_conv_w, ssm_conv_b, ssm_a_log, ssm_d, ssm_norm_w, w_proj_m, w_proj_s, w_out,
           final_w):
    b = x.shape[0]
    assert norm_w.shape[0] == 1, "the residual + final-norm epilogue is fused for a single layer"
    c_pad = jnp.pad(c, ((0, (-b) % SUBLANES), (0, 0)))
    return _layer(x, c_pad, norm_w[0], ada_w[0], ada_b[0], w_in[0], b_in[0], ml_conv_w[0],
                  ml_conv_b[0], ml_norm_w[0], ssm_conv_w[0], ssm_conv_b[0], ssm_a_log[0],
                  ssm_d[0], ssm_norm_w[0], w_proj_m[0], w_proj_s[0], w_out[0], final_w)
```

```python
import math

import jax
import jax.numpy as jnp
from jax import lax
from jax.experimental import pallas as pl
from jax.experimental.pallas import tpu as pltpu

F32 = jnp.float32
BF16 = jnp.bfloat16

D_MODEL = 1024
CHUNK = 64
ML_CHUNK = 128
SCAN_ROWS = 256
CONV_K = 4
EPS = 1e-6
ML_HEADS = 8
ML_INNER = 2 * D_MODEL
ML_DV = ML_INNER // ML_HEADS
ML_DQK = ML_DV // 2
ML_QK2 = 2 * ML_HEADS * ML_DQK
QK_SCALE = ML_DQK ** -0.5
SSM_INNER = 2 * D_MODEL
SSM_HEADDIM = 64
SSM_HEADS = SSM_INNER // SSM_HEADDIM
SSM_GROUPS = 4
SSM_STATE = 128
SSM_BC = 2 * SSM_GROUPS * SSM_STATE
SSM_XBC = SSM_INNER + SSM_BC
HEADS_PER_GROUP = SSM_HEADS // SSM_GROUPS

SEG_SIZES = (ML_QK2, ML_INNER, ML_INNER, ML_INNER, ML_HEADS, ML_HEADS,
             SSM_XBC, SSM_INNER, SSM_HEADS, 2 * D_MODEL)
SEG_OFF = tuple(sum(SEG_SIZES[:i]) for i in range(len(SEG_SIZES)))

SEG_W = 2048
CONV_QK, CONV_XS = 0, 1
CONV_BC_BLOCK = 2 * SEG_W // SSM_BC
CONV_WIDTH = 2 * SEG_W + SSM_BC
PLAIN_V, PLAIN_O, PLAIN_ZM, PLAIN_ZS, PLAIN_MERGE = range(5)
PLAIN_WIDTH = 5 * SEG_W

LANES = 128
GATE_W = 2 * LANES
GATE_DT = 16

SUBLANES = 8
MXU_WIDTH = 256
INPROJ_PIECE = MXU_WIDTH
VMEM_LIMIT = 48 * 1024 * 1024


def _sigmoid(x):
    return 0.5 * jnp.tanh(0.5 * x) + 0.5


def _silu(x):
    return x * _sigmoid(x)


def _softplus(x):
    return jnp.maximum(x, 0.0) + jnp.log1p(jnp.exp(-jnp.abs(x)))


def _log_sigmoid(x):
    return jnp.minimum(x, 0.0) - jnp.log1p(jnp.exp(-jnp.abs(x)))


def _split2(x):
    hi = x.astype(BF16)
    lo = (x - hi.astype(F32)).astype(BF16)
    return hi, lo


def _split3(x):
    hi = x.astype(BF16)
    r1 = x - hi.astype(F32)
    mid = r1.astype(BF16)
    lo = (r1 - mid.astype(F32)).astype(BF16)
    return hi, mid, lo


def _dot3(a, b):
    a_hi, a_lo = _split2(a)
    b_hi, b_lo = _split2(b)
    acc = jnp.dot(a_hi, b_hi, preferred_element_type=F32)
    acc = acc + jnp.dot(a_hi, b_lo, preferred_element_type=F32)
    return acc + jnp.dot(a_lo, b_hi, preferred_element_type=F32)


def _cumsum_rows(x, tri):
    hi, mid, lo = _split3(x)
    return (jnp.dot(tri, hi, preferred_element_type=F32)
            + jnp.dot(tri, mid, preferred_element_type=F32)
            + jnp.dot(tri, lo, preferred_element_type=F32))


def _shift_rows(x, d, fill):
    n, w = x.shape
    xe = jnp.concatenate([jnp.full((SUBLANES, w), fill, x.dtype), x], axis=0)
    return xe[SUBLANES - d:SUBLANES - d + n, :]


def _cummax_rows(x):
    n, w = x.shape
    d = 1
    while d < n:
        if d < SUBLANES:
            sh = _shift_rows(x, d, -jnp.inf)
        else:
            sh = jnp.concatenate([jnp.full((d, w), -jnp.inf, x.dtype), x[:n - d, :]], axis=0)
        x = jnp.maximum(x, sh)
        d *= 2
    return x


def _lower_tri(n, block=None):
    r = lax.broadcasted_iota(jnp.int32, (n, n), 0)
    c = lax.broadcasted_iota(jnp.int32, (n, n), 1)
    keep = c <= r
    if block is not None and block < n:
        keep = keep & (c >= (r // block) * block)
    return keep.astype(BF16)


def _adaln(x_ref, nw_ref, sc_ref, sh_ref):
    xf = x_ref[...]
    ms = jnp.mean(xf * xf, axis=-1, keepdims=True)
    y = xf * lax.rsqrt(ms + EPS) * nw_ref[...]
    return y * (1.0 + sc_ref[...]) + sh_ref[...]


def _mod_kernel(c_ref, w_ref, b_ref, o_ref):
    o_ref[...] = _dot3(_silu(c_ref[...]), w_ref[...]) + b_ref[...]


def _mod(c_pad, ada_w, ada_b):
    rows = c_pad.shape[0]
    n = ada_w.shape[1]
    tn = D_MODEL
    return pl.pallas_call(
        _mod_kernel,
        grid=(n // tn,),
        in_specs=[pl.BlockSpec((rows, D_MODEL), lambda j: (0, 0)),
                  pl.BlockSpec((D_MODEL, tn), lambda j: (0, j)),
                  pl.BlockSpec((1, tn), lambda j: (0, j))],
        out_specs=pl.BlockSpec((rows, tn), lambda j: (0, j)),
        out_shape=jax.ShapeDtypeStruct((rows, n), F32),
        compiler_params=pltpu.CompilerParams(dimension_semantics=("arbitrary",),
                                             vmem_limit_bytes=VMEM_LIMIT),
        name="mod",
    )(c_pad, ada_w, ada_b)


def _shift_down(xe, s):
    n = xe.shape[0] - SUBLANES
    w = xe.shape[1]
    sub = lax.broadcasted_iota(jnp.int32, (n, w), 0) % SUBLANES
    mixed = jnp.where(sub >= SUBLANES - s, xe[:n, :], xe[SUBLANES:, :])
    return pltpu.roll(mixed.reshape(-1, SUBLANES, w), s, axis=1).reshape(n, w)


def _inproj_kernel(x_ref, nw_ref, sc_ref, sh_ref, wc_ref, bc_ref, cw_ref, cb_ref,
                   wp_ref, bp_ref, wg_ref, bg_ref, oc_ref, op_ref, og_ref, u_ref, tail_ref,
                   acc_ref):
    i = pl.program_id(1)
    j = pl.program_id(2)
    tm = x_ref.shape[0]

    @pl.when(j == 0)
    def _():
        u = _adaln(x_ref, nw_ref, sc_ref, sh_ref)
        u_ref[...] = u.astype(BF16)
        og_ref[...] = _dot3(u, wg_ref[...]) + bg_ref[...]

    @pl.when(i == 0)
    def _():
        tail_ref[j] = jnp.broadcast_to(-bc_ref[...], tail_ref.shape[1:])

    pieces = wc_ref.shape[1] // INPROJ_PIECE
    pw = wp_ref.shape[1] // pieces

    def matmuls(p):
        cs = slice(p * INPROJ_PIECE, (p + 1) * INPROJ_PIECE)
        ps = slice(p * pw, (p + 1) * pw)
        u = u_ref[...]
        acc_ref[:SUBLANES, cs] = tail_ref[j, :, cs]
        acc_ref[SUBLANES:, cs] = jnp.dot(u, wc_ref[:, cs], preferred_element_type=F32)
        op_ref[:, ps] = (jnp.dot(u, wp_ref[:, ps], preferred_element_type=F32)
                         + bp_ref[:, ps]).astype(op_ref.dtype)

    def epilogue(p):
        cs = slice(p * INPROJ_PIECE, (p + 1) * INPROJ_PIECE)
        xe = acc_ref[:, cs]
        acc = xe[SUBLANES:, :]
        tail_ref[j, :, cs] = xe[tm:, :]
        yh = cb_ref[:, cs] + acc * cw_ref[CONV_K - 1:CONV_K, cs]
        for k in range(CONV_K - 1):
            yh = yh + _shift_down(xe, CONV_K - 1 - k) * cw_ref[k:k + 1, cs]
        oc_ref[:, cs] = (yh * (jnp.tanh(yh) + 1.0)).astype(oc_ref.dtype)

    for p in range(pieces + 1):
        if p < pieces:
            matmuls(p)
        if p > 0:
            epilogue(p - 1)


def _inproj(x, norm_w, scale, shift, w_conv, b_conv, conv_w, conv_b, w_plain, b_plain,
            w_gate, b_gate, *, tm, ncol):
    bsz, s, _ = x.shape
    tc = CONV_WIDTH // ncol
    tp = PLAIN_WIDTH // ncol
    row = lambda bi, i, j: (bi, i, 0)
    col = lambda bi, i, j: (0, j)
    fix = lambda bi, i, j: (0, 0)
    per_b = lambda bi, i, j: (bi, 0, 0)
    tile = lambda bi, i, j: (bi, i, j)
    return pl.pallas_call(
        _inproj_kernel,
        grid=(bsz, s // tm, ncol),
        in_specs=[
            pl.BlockSpec((None, tm, D_MODEL), row),
            pl.BlockSpec((1, D_MODEL), fix),
            pl.BlockSpec((None, 1, D_MODEL), per_b),
            pl.BlockSpec((None, 1, D_MODEL), per_b),
            pl.BlockSpec((D_MODEL, tc), col),
            pl.BlockSpec((1, tc), col),
            pl.BlockSpec((CONV_K, tc), col),
            pl.BlockSpec((1, tc), col),
            pl.BlockSpec((D_MODEL, tp), col),
            pl.BlockSpec((1, tp), col),
            pl.BlockSpec((D_MODEL, GATE_W), fix),
            pl.BlockSpec((1, GATE_W), fix),
        ],
        out_specs=[
            pl.BlockSpec((None, tm, tc), tile),
            pl.BlockSpec((None, tm, tp), tile),
            pl.BlockSpec((None, tm, GATE_W), row),
        ],
        out_shape=[jax.ShapeDtypeStruct((bsz, s, CONV_WIDTH), BF16),
                   jax.ShapeDtypeStruct((bsz, s, PLAIN_WIDTH), BF16),
                   jax.ShapeDtypeStruct((bsz, s, GATE_W), F32)],
        scratch_shapes=[pltpu.VMEM((tm, D_MODEL), BF16),
                        pltpu.VMEM((ncol, SUBLANES, tc), F32),
                        pltpu.VMEM((SUBLANES + tm, tc), F32)],
        compiler_params=pltpu.CompilerParams(
            dimension_semantics=("parallel", "arbitrary", "arbitrary"),
            vmem_limit_bytes=VMEM_LIMIT),
        name="inproj",
    )(x, norm_w, scale, shift, w_conv, b_conv, conv_w, conv_b, w_plain, b_plain,
      w_gate, b_gate)


def _mlstm_kernel(qk_ref, v_ref, g_ref, h_ref, cn_scr, m_scr):
    @pl.when(pl.program_id(1) == 0)
    def _():
        cn_scr[...] = jnp.zeros_like(cn_scr)
        m_scr[...] = jnp.zeros_like(m_scr)

    n = ML_CHUNK
    nrows = qk_ref.shape[0]
    row = lax.broadcasted_iota(jnp.int32, (n, n), 0)
    col = lax.broadcasted_iota(jnp.int32, (n, n), 1)
    causal = col <= row

    a_all = _cumsum_rows(_log_sigmoid(g_ref[:, LANES:]), _lower_tri(nrows, n))
    r_all = g_ref[:, :LANES] - a_all
    r_t_all = (r_all + math.log(QK_SCALE)).T

    for ch in range(nrows // n):
        rows = slice(ch * n, (ch + 1) * n)
        a = a_all[rows, :]
        r = r_all[rows, :]
        m_prev = m_scr[...]
        mm = jnp.maximum(_cummax_rows(r), m_prev)
        w_inter = jnp.exp(m_prev - mm)
        floor = jnp.exp(-(a + mm))
        mm_last = mm[n - 1:n, :]
        wk_all = jnp.exp(r - mm_last)
        m_scr[...] = a[n - 1:n, :] + mm_last
        w_inter_q = w_inter * QK_SCALE
        r_t = r_t_all[:, rows]

        def q_of(h):
            return qk_ref[rows, h * ML_DQK:(h + 1) * ML_DQK]

        def k_of(h):
            return qk_ref[rows, ML_HEADS * ML_DQK + h * ML_DQK:ML_HEADS * ML_DQK + (h + 1) * ML_DQK]

        def v_of(h):
            return v_ref[rows, h * ML_DV:(h + 1) * ML_DV]

        ones = jnp.ones((n, LANES), BF16)
        for h in range(ML_HEADS):
            q = q_of(h)
            v_ext = jnp.concatenate([v_of(h), ones], axis=1)
            cn_prev = cn_scr[h]
            w_in = w_inter_q[:, h:h + 1]

            w_intra = jnp.exp(jnp.where(causal, r_t[h:h + 1, :] - mm[:, h:h + 1], -jnp.inf))
            sc = lax.dot_general(q, k_of(h), (((1,), (1,)), ((), ())),
                                 preferred_element_type=F32) * w_intra
            numden = (jnp.dot(sc.astype(BF16), v_ext, preferred_element_type=F32)
                      + w_in * jnp.dot(q, cn_prev.astype(BF16), preferred_element_type=F32))
            den = numden[:, ML_DV:]
            inv = 1.0 / jnp.maximum(jnp.abs(den), floor[:, h:h + 1])
            hval = numden[:, :ML_DV] * jnp.concatenate([inv] * (ML_DV // LANES), axis=1)
            h_ref[rows, h * ML_DV:(h + 1) * ML_DV] = hval.astype(h_ref.dtype)

            decay = w_inter[n - 1:n, h:h + 1]
            kw = k_of(h).astype(F32) * wk_all[:, h:h + 1]
            cn_scr[h] = decay * cn_prev + jnp.dot(kw.T.astype(BF16), v_ext,
                                                  preferred_element_type=F32)


def _ssd_kernel(xs_ref, bc_ref, g_ref, acoef_ref, dskip_ref, y_ref, st_scr):
    @pl.when(pl.program_id(1) == 0)
    def _():
        st_scr[...] = jnp.zeros_like(st_scr)

    ngs = SSM_GROUPS * SSM_STATE
    pair_w = 2 * SSM_HEADDIM
    nrows = xs_ref.shape[0]
    nch = nrows // CHUNK
    dt_all = _softplus(g_ref[:, :LANES])
    a_all = _cumsum_rows(dt_all * acoef_ref[...], _lower_tri(nrows, CHUNK))

    def doubled_t(x):
        x3 = x.reshape(nch, CHUNK, LANES)
        return jnp.concatenate([x3, x3], axis=1).reshape(2 * nrows, LANES).T

    a2_t_all = doubled_t(a_all)
    dt2_t_all = doubled_t(dt_all)
    lane = lax.broadcasted_iota(jnp.int32, (1, pair_w), 1)
    left = lane < SSM_HEADDIM
    left_bf = left.astype(BF16)
    right_bf = (lane >= SSM_HEADDIM).astype(BF16)
    row = lax.broadcasted_iota(jnp.int32, (CHUNK, 2 * CHUNK), 0)
    col = lax.broadcasted_iota(jnp.int32, (CHUNK, 2 * CHUNK), 1)
    causal2 = jnp.bitwise_and(col, CHUNK - 1) <= row
    eye2 = jnp.bitwise_and(col, CHUNK - 1) == row

    for ch in range(nch):
        rows = slice(ch * CHUNK, (ch + 1) * CHUNK)
        dt = dt_all[rows, :]
        a = a_all[rows, :]
        a2_t = a2_t_all[:, ch * 2 * CHUNK:(ch + 1) * 2 * CHUNK]
        dt2_t = dt2_t_all[:, ch * 2 * CHUNK:(ch + 1) * 2 * CHUNK]

        for gi in range(SSM_GROUPS):
            bg = bc_ref[rows, gi * SSM_STATE:(gi + 1) * SSM_STATE]
            cg = bc_ref[rows, ngs + gi * SSM_STATE:ngs + (gi + 1) * SSM_STATE]
            b2 = jnp.concatenate([bg, bg], axis=0)
            scores2 = lax.dot_general(cg, b2, (((1,), (1,)), ((), ())),
                                      preferred_element_type=F32)
            st = st_scr[gi]
            y_inter = jnp.dot(cg, st.astype(BF16), preferred_element_type=F32)
            bg_t = bg.astype(F32).T.astype(BF16)
            for p in range(HEADS_PER_GROUP // 2):
                pair = gi * (HEADS_PER_GROUP // 2) + p
                la = GATE_DT + 2 * pair
                lb = la + 1
                slab = slice(pair * pair_w, (pair + 1) * pair_w)
                sl_g = slice(p * pair_w, (p + 1) * pair_w)
                a_col2 = jnp.where(left, a[:, la:la + 1], a[:, lb:lb + 1])
                a_row2 = jnp.where(left, a2_t[la:la + 1, :], a2_t[lb:lb + 1, :])
                dt_row2 = jnp.where(left, dt2_t[la:la + 1, :], dt2_t[lb:lb + 1, :])
                a_last2 = a_col2[CHUNK - 1:CHUNK, :]

                decay2 = jnp.exp(jnp.where(causal2, a_col2 - a_row2, -jnp.inf))
                mix2 = (decay2 * scores2 * dt_row2).astype(BF16)
                wts_row2 = jnp.exp(a_last2 - a_row2) * dt_row2
                diag2 = jnp.where(eye2, wts_row2, 0.0).astype(BF16)
                xp_bf = xs_ref[rows, slab]
                xp_bd = jnp.concatenate([xp_bf * left_bf, xp_bf * right_bf], axis=0)
                both = jnp.dot(jnp.concatenate([mix2, diag2], axis=0), xp_bd,
                               preferred_element_type=F32)
                y = (both[:CHUNK, :]
                     + jnp.exp(a_col2) * y_inter[:, sl_g]
                     + dskip_ref[:, slab] * xp_bf.astype(F32))
                y_ref[rows, slab] = y.astype(y_ref.dtype)

                xw = both[CHUNK:, :].astype(BF16)
                st_scr[gi, :, sl_g] = (jnp.exp(a_last2) * st[:, sl_g]
                                       + jnp.dot(bg_t, xw, preferred_element_type=F32))


def _scans_kernel(qk_ref, v_ref, g_ref, xs_ref, bc_ref, acoef_ref, dskip_ref, h_ref, y_ref,
                  cn_scr, m_scr, st_scr):
    _mlstm_kernel(qk_ref, v_ref, g_ref, h_ref, cn_scr, m_scr)
    _ssd_kernel(xs_ref, bc_ref, g_ref, acoef_ref, dskip_ref, y_ref, st_scr)


def _scans(conv_out, plain_out, gates, acoef_row, dskip_row, *, rows):
    b, s, _ = conv_out.shape
    blk = lambda bi, c: (bi, c, 0)
    return pl.pallas_call(
        _scans_kernel,
        grid=(b, s // rows),
        in_specs=[
            pl.BlockSpec((None, rows, SEG_W), lambda bi, c: (bi, c, CONV_QK)),
            pl.BlockSpec((None, rows, SEG_W), lambda bi, c: (bi, c, PLAIN_V)),
            pl.BlockSpec((None, rows, GATE_W), blk),
            pl.BlockSpec((None, rows, SEG_W), lambda bi, c: (bi, c, CONV_XS)),
            pl.BlockSpec((None, rows, SSM_BC), lambda bi, c: (bi, c, CONV_BC_BLOCK)),
            pl.BlockSpec((1, LANES), lambda bi, c: (0, 0)),
            pl.BlockSpec((1, SSM_INNER), lambda bi, c: (0, 0)),
        ],
        out_specs=[pl.BlockSpec((None, rows, ML_INNER), blk),
                   pl.BlockSpec((None, rows, SSM_INNER), blk)],
        out_shape=[jax.ShapeDtypeStruct((b, s, ML_INNER), BF16),
                   jax.ShapeDtypeStruct((b, s, SSM_INNER), BF16)],
        scratch_shapes=[pltpu.VMEM((ML_HEADS, ML_DQK, ML_DV + LANES), F32),
                        pltpu.VMEM((1, LANES), F32),
                        pltpu.VMEM((SSM_GROUPS, SSM_STATE, HEADS_PER_GROUP * SSM_HEADDIM), F32)],
        compiler_params=pltpu.CompilerParams(dimension_semantics=("parallel", "arbitrary"),
                                             vmem_limit_bytes=VMEM_LIMIT),
        name="scans",
    )(conv_out, plain_out, gates, conv_out, conv_out, acoef_row, dskip_row)


def _outproj_kernel(h_ref, o_ref, zm_ref, ys_ref, zs_ref, mg_ref, x_ref, gate_ref,
                    wm_ref, ws_ref, wo_ref, fw_ref, out_ref):
    ym = []
    for hd in range(ML_HEADS):
        sl = slice(hd * ML_DV, (hd + 1) * ML_DV)
        hh = h_ref[:, sl].astype(F32)
        rs = lax.rsqrt(jnp.mean(hh * hh, axis=-1, keepdims=True) + EPS)
        zh = zm_ref[:, sl].astype(F32)
        gates = (jnp.tanh(o_ref[:, sl].astype(F32)) + 1.0) * (jnp.tanh(zh) + 1.0)
        ym.append((gates * (zh * hh) * rs).astype(BF16))
    pm = jnp.dot(jnp.concatenate(ym, axis=1), wm_ref[...], preferred_element_type=F32)
    yn = []
    gw = SSM_INNER // SSM_GROUPS
    for gi in range(SSM_GROUPS):
        sl = slice(gi * gw, (gi + 1) * gw)
        zh = zs_ref[:, sl].astype(F32)
        t = ys_ref[:, sl].astype(F32) * zh * (jnp.tanh(zh) + 1.0)
        rs = lax.rsqrt(jnp.mean(t * t, axis=-1, keepdims=True) + EPS)
        yn.append((t * rs).astype(BF16))
    ps = jnp.dot(jnp.concatenate(yn, axis=1), ws_ref[...], preferred_element_type=F32)
    gm = jnp.tanh(mg_ref[:, :D_MODEL].astype(F32)) + 1.0
    gs = jnp.tanh(mg_ref[:, D_MODEL:].astype(F32)) + 1.0
    merged = (gm * pm + gs * ps).astype(BF16)
    r = jnp.dot(merged, wo_ref[...], preferred_element_type=F32)
    xo = x_ref[...] + gate_ref[...] * r
    ms = jnp.mean(xo * xo, axis=-1, keepdims=True)
    out_ref[...] = xo * lax.rsqrt(ms + EPS) * fw_ref[...]


def _outproj(h_m, y_s, plain_out, x, gate, w_m, w_s, w_o, final_w, *, tm):
    b, s, _ = x.shape

    def seg(k):
        return pl.BlockSpec((None, tm, SEG_W), lambda bi, i: (bi, i, k))

    def full(shape):
        return pl.BlockSpec(shape, lambda bi, i: (0,) * len(shape), pipeline_mode=pl.Buffered(1))

    return pl.pallas_call(
        _outproj_kernel,
        grid=(b, s // tm),
        in_specs=[
            pl.BlockSpec((None, tm, ML_INNER), lambda bi, i: (bi, i, 0)),
            seg(PLAIN_O), seg(PLAIN_ZM),
            pl.BlockSpec((None, tm, SSM_INNER), lambda bi, i: (bi, i, 0)),
            seg(PLAIN_ZS), seg(PLAIN_MERGE),
            pl.BlockSpec((None, tm, D_MODEL), lambda bi, i: (bi, i, 0)),
            pl.BlockSpec((None, 1, D_MODEL), lambda bi, i: (bi, 0, 0)),
            full((ML_INNER, D_MODEL)), full((SSM_INNER, D_MODEL)), full((D_MODEL, D_MODEL)),
            full((1, D_MODEL)),
        ],
        out_specs=pl.BlockSpec((None, tm, D_MODEL), lambda bi, i: (bi, i, 0)),
        out_shape=jax.ShapeDtypeStruct((b, s, D_MODEL), F32),
        compiler_params=pltpu.CompilerParams(dimension_semantics=("parallel", "parallel"),
                                             vmem_limit_bytes=VMEM_LIMIT),
        name="outproj",
    )(h_m, plain_out, plain_out, y_s, plain_out, plain_out, x, gate, w_m, w_s, w_o, final_w)


def _seg(w, k):
    return w[..., SEG_OFF[k]:SEG_OFF[k] + SEG_SIZES[k]]


def _gate_cols(i_cols, f_cols, dt_cols):
    lead = i_cols.shape[:-1]

    def z(n):
        return jnp.zeros(lead + (n,), i_cols.dtype)

    return jnp.concatenate(
        [i_cols, z(GATE_DT - ML_HEADS), dt_cols, z(LANES - GATE_DT - SSM_HEADS),
         f_cols, z(LANES - ML_HEADS)], axis=-1)


def _layer(x, c_pad, norm_w, ada_w, ada_b, w_in, b_in, ml_conv_w, ml_conv_b, ml_norm_w,
           ssm_conv_w, ssm_conv_b, ssm_a_log, ssm_d, ssm_norm_w, w_proj_m, w_proj_s, w_out,
           final_w):
    b, s, _ = x.shape
    mod = _mod(c_pad, ada_w, ada_b.reshape(1, -1))[:b]
    shift = mod[:, :D_MODEL].reshape(b, 1, D_MODEL)
    scale = mod[:, D_MODEL:2 * D_MODEL].reshape(b, 1, D_MODEL)
    gate = mod[:, 2 * D_MODEL:].reshape(b, 1, D_MODEL)
    norm_w = norm_w.reshape(1, D_MODEL)

    w_conv = jnp.concatenate([_seg(w_in, 0).astype(BF16), _seg(w_in, 6).astype(BF16)], axis=1)
    b_conv = jnp.concatenate([_seg(b_in, 0), _seg(b_in, 6)]).reshape(1, CONV_WIDTH)
    conv_w = jnp.concatenate([ml_conv_w, ssm_conv_w], axis=1)
    conv_b = jnp.concatenate([ml_conv_b, ssm_conv_b]).reshape(1, CONV_WIDTH)
    conv_b_half = 0.5 * (conv_b + b_conv * jnp.sum(conv_w, axis=0, keepdims=True))
    conv_w_half = 0.5 * conv_w
    w_plain = jnp.concatenate(
        [_seg(w_in, 1).astype(BF16)] + [(0.5 * _seg(w_in, k)).astype(BF16) for k in (2, 3, 7, 9)],
        axis=1)
    b_plain = jnp.concatenate([_seg(b_in, 1), 0.5 * _seg(b_in, 2), 0.5 * _seg(b_in, 3),
                               0.5 * _seg(b_in, 7), 0.5 * _seg(b_in, 9)]).reshape(1, PLAIN_WIDTH)
    w_gate = _gate_cols(_seg(w_in, 4), _seg(w_in, 5), _seg(w_in, 8))
    b_gate = _gate_cols(_seg(b_in, 4), _seg(b_in, 5), _seg(b_in, 8)).reshape(1, GATE_W)

    conv_out, plain_out, gates = _inproj(
        x, norm_w, scale, shift, w_conv, b_conv, conv_w_half, conv_b_half,
        w_plain, b_plain, w_gate, b_gate, tm=min(512, s), ncol=4)

    acoef_row = jnp.pad(-jnp.exp(ssm_a_log.astype(F32)),
                        (GATE_DT, LANES - GATE_DT - SSM_HEADS)).reshape(1, LANES)
    dskip_row = jnp.repeat(ssm_d, SSM_HEADDIM).reshape(1, SSM_INNER)
    h_m, y_s = _scans(conv_out, plain_out, gates, acoef_row, dskip_row, rows=min(SCAN_ROWS, s))

    w_m = ((0.5 * ml_norm_w)[:, None] * w_proj_m).astype(BF16)
    w_s = (ssm_norm_w[:, None] * w_proj_s).astype(BF16)
    w_o = (0.5 * w_out).astype(BF16)
    return _outproj(h_m, y_s, plain_out, x, gate, w_m, w_s, w_o, final_w.reshape(1, D_MODEL),
                    tm=min(512, s))


def kernel(x, c, norm_w, ada_w, ada_b, w_in, b_in, ml_conv_w, ml_conv_b, ml_norm_w,
           ssm_conv_w, ssm_conv_b, ssm_a_log, ssm_d, ssm_norm_w, w_proj_m, w_proj_s, w_out,
           final_w):
    b = x.shape[0]
    assert norm_w.shape[0] == 1, "the residual + final-norm epilogue is fused for a single layer"
    c_pad = jnp.pad(c, ((0, (-b) % SUBLANES), (0, 0)))
    return _layer(x, c_pad, norm_w[0], ada_w[0], ada_b[0], w_in[0], b_in[0], ml_conv_w[0],
                  ml_conv_b[0], ml_norm_w[0], ssm_conv_w[0], ssm_conv_b[0], ssm_a_log[0],
                  ssm_d[0], ssm_norm_w[0], w_proj_m[0], w_proj_s[0], w_out[0], final_w)
```

```python
import functools
import math

import jax
import jax.numpy as jnp
from jax import lax
from jax.experimental import pallas as pl
from jax.experimental.pallas import tpu as pltpu

F32 = jnp.float32
BF16 = jnp.bfloat16

D_MODEL = 1024
CHUNK = 64
ML_CHUNK = 128
SCAN_ROWS = 512
CONV_K = 4
EPS = 1e-6
ML_HEADS = 8
ML_INNER = 2 * D_MODEL
ML_DV = ML_INNER // ML_HEADS
ML_DQK = ML_DV // 2
ML_QK2 = 2 * ML_HEADS * ML_DQK
QK_SCALE = ML_DQK ** -0.5
SSM_INNER = 2 * D_MODEL
SSM_HEADDIM = 64
SSM_HEADS = SSM_INNER // SSM_HEADDIM
SSM_GROUPS = 4
SSM_STATE = 128
SSM_BC = 2 * SSM_GROUPS * SSM_STATE
SSM_XBC = SSM_INNER + SSM_BC
HEADS_PER_GROUP = SSM_HEADS // SSM_GROUPS

SEG_SIZES = (ML_QK2, ML_INNER, ML_INNER, ML_INNER, ML_HEADS, ML_HEADS,
             SSM_XBC, SSM_INNER, SSM_HEADS, 2 * D_MODEL)
SEG_OFF = tuple(sum(SEG_SIZES[:i]) for i in range(len(SEG_SIZES)))

SEG_W = 2048
CONV_QK, CONV_XS = 0, 1
CONV_BC_BLOCK = 2 * SEG_W // SSM_BC
CONV_WIDTH = 2 * SEG_W + SSM_BC
PLAIN_V, PLAIN_O, PLAIN_ZM, PLAIN_ZS, PLAIN_MERGE = range(5)
PLAIN_WIDTH = 5 * SEG_W

LANES = 128
GATE_W = 2 * LANES
GATE_DT = 16

SUBLANES = 8
MXU_WIDTH = 256
INPROJ_PIECE = MXU_WIDTH
VMEM_LIMIT = 48 * 1024 * 1024


def _sigmoid(x):
    return 0.5 * jnp.tanh(0.5 * x) + 0.5


def _silu(x):
    return x * _sigmoid(x)


def _softplus(x):
    return jnp.maximum(x, 0.0) + jnp.log1p(jnp.exp(-jnp.abs(x)))


def _log_sigmoid(x):
    return jnp.minimum(x, 0.0) - jnp.log1p(jnp.exp(-jnp.abs(x)))


def _split2(x):
    hi = x.astype(BF16)
    lo = (x - hi.astype(F32)).astype(BF16)
    return hi, lo


def _split3(x):
    hi = x.astype(BF16)
    r1 = x - hi.astype(F32)
    mid = r1.astype(BF16)
    lo = (r1 - mid.astype(F32)).astype(BF16)
    return hi, mid, lo


def _dot_nt(a, b):
    return lax.dot_general(a, b, (((1,), (1,)), ((), ())), preferred_element_type=F32)


def _dot3(a, b, dot=functools.partial(jnp.dot, preferred_element_type=F32)):
    a_hi, a_lo = _split2(a)
    b_hi, b_lo = _split2(b)
    return dot(a_hi, b_hi) + dot(a_hi, b_lo) + dot(a_lo, b_hi)


def _cumsum_rows(x, tri):
    hi, mid, lo = _split3(x)
    return (jnp.dot(tri, hi, preferred_element_type=F32)
            + jnp.dot(tri, mid, preferred_element_type=F32)
            + jnp.dot(tri, lo, preferred_element_type=F32))


def _shift_rows(x, d, fill):
    n, w = x.shape
    xe = jnp.concatenate([jnp.full((SUBLANES, w), fill, x.dtype), x], axis=0)
    return xe[SUBLANES - d:SUBLANES - d + n, :]


def _cummax_rows(x):
    n, w = x.shape
    d = 1
    while d < n:
        if d < SUBLANES:
            sh = _shift_rows(x, d, -jnp.inf)
        else:
            sh = jnp.concatenate([jnp.full((d, w), -jnp.inf, x.dtype), x[:n - d, :]], axis=0)
        x = jnp.maximum(x, sh)
        d *= 2
    return x


def _lower_tri(n, block=None):
    r = lax.broadcasted_iota(jnp.int32, (n, n), 0)
    c = lax.broadcasted_iota(jnp.int32, (n, n), 1)
    keep = c <= r
    if block is not None and block < n:
        keep = keep & (c >= (r // block) * block)
    return keep.astype(BF16)


def _adaln(x_ref, nw_ref, sc_ref, sh_ref):
    xf = x_ref[...]
    ms = jnp.mean(xf * xf, axis=-1, keepdims=True)
    y = xf * lax.rsqrt(ms + EPS) * nw_ref[...]
    return y * (1.0 + sc_ref[...]) + sh_ref[...]


def _mod_kernel(c_ref, w_ref, b_ref, o_ref):
    o_ref[...] = _dot3(_silu(c_ref[...]), w_ref[...]) + b_ref[...]


def _mod(c_pad, ada_w, ada_b):
    rows = c_pad.shape[0]
    n = ada_w.shape[1]
    tn = D_MODEL
    return pl.pallas_call(
        _mod_kernel,
        grid=(n // tn,),
        in_specs=[pl.BlockSpec((rows, D_MODEL), lambda j: (0, 0)),
                  pl.BlockSpec((D_MODEL, tn), lambda j: (0, j)),
                  pl.BlockSpec((1, tn), lambda j: (0, j))],
        out_specs=pl.BlockSpec((rows, tn), lambda j: (0, j)),
        out_shape=jax.ShapeDtypeStruct((rows, n), F32),
        compiler_params=pltpu.CompilerParams(dimension_semantics=("arbitrary",),
                                             vmem_limit_bytes=VMEM_LIMIT),
        name="mod",
    )(c_pad, ada_w, ada_b)


def _shift_down(xe, s):
    n = xe.shape[0] - SUBLANES
    w = xe.shape[1]
    sub = lax.broadcasted_iota(jnp.int32, (n, w), 0) % SUBLANES
    mixed = jnp.where(sub >= SUBLANES - s, xe[:n, :], xe[SUBLANES:, :])
    return pltpu.roll(mixed.reshape(-1, SUBLANES, w), s, axis=1).reshape(n, w)


def _inproj_kernel(x_ref, nw_ref, sc_ref, sh_ref, wc_ref, bc_ref, cw_ref, cb_ref,
                   wp_ref, bp_ref, wg_ref, bg_ref, oc_ref, op_ref, og_ref, u_ref, tail_ref,
                   acc_ref):
    i = pl.program_id(1)
    j = pl.program_id(2)
    tm = x_ref.shape[0]

    @pl.when(j == 0)
    def _():
        u = _adaln(x_ref, nw_ref, sc_ref, sh_ref)
        u_ref[...] = u.astype(BF16)
        og_ref[...] = _dot3(u, wg_ref[...], _dot_nt) + bg_ref[...]

    @pl.when(i == 0)
    def _():
        tail_ref[j] = jnp.broadcast_to(-bc_ref[...], tail_ref.shape[1:])

    pieces = wc_ref.shape[1] // INPROJ_PIECE
    pw = wp_ref.shape[1] // pieces

    def matmuls(p):
        cs = slice(p * INPROJ_PIECE, (p + 1) * INPROJ_PIECE)
        ps = slice(p * pw, (p + 1) * pw)
        u = u_ref[...]
        acc_ref[:SUBLANES, cs] = tail_ref[j, :, cs]
        acc_ref[SUBLANES:, cs] = jnp.dot(u, wc_ref[:, cs], preferred_element_type=F32)
        op_ref[:, ps] = (jnp.dot(u, wp_ref[:, ps], preferred_element_type=F32)
                         + bp_ref[:, ps]).astype(op_ref.dtype)

    def epilogue(p):
        cs = slice(p * INPROJ_PIECE, (p + 1) * INPROJ_PIECE)
        xe = acc_ref[:, cs]
        acc = xe[SUBLANES:, :]
        tail_ref[j, :, cs] = xe[tm:, :]
        yh = cb_ref[:, cs] + acc * cw_ref[CONV_K - 1:CONV_K, cs]
        for k in range(CONV_K - 1):
            yh = yh + _shift_down(xe, CONV_K - 1 - k) * cw_ref[k:k + 1, cs]
        oc_ref[:, cs] = (yh * (jnp.tanh(yh) + 1.0)).astype(oc_ref.dtype)

    for p in range(pieces + 1):
        if p < pieces:
            matmuls(p)
        if p > 0:
            epilogue(p - 1)


def _inproj(x, norm_w, scale, shift, w_conv, b_conv, conv_w, conv_b, w_plain, b_plain,
            w_gate, b_gate, *, tm, ncol):
    bsz, s, _ = x.shape
    tc = CONV_WIDTH // ncol
    tp = PLAIN_WIDTH // ncol
    row = lambda bi, i, j: (bi, i, 0)
    col = lambda bi, i, j: (0, j)
    fix = lambda bi, i, j: (0, 0)
    per_b = lambda bi, i, j: (bi, 0, 0)
    tile = lambda bi, i, j: (bi, i, j)
    return pl.pallas_call(
        _inproj_kernel,
        grid=(bsz, s // tm, ncol),
        in_specs=[
            pl.BlockSpec((None, tm, D_MODEL), row),
            pl.BlockSpec((1, D_MODEL), fix),
            pl.BlockSpec((None, 1, D_MODEL), per_b),
            pl.BlockSpec((None, 1, D_MODEL), per_b),
            pl.BlockSpec((D_MODEL, tc), col),
            pl.BlockSpec((1, tc), col),
            pl.BlockSpec((CONV_K, tc), col),
            pl.BlockSpec((1, tc), col),
            pl.BlockSpec((D_MODEL, tp), col),
            pl.BlockSpec((1, tp), col),
            pl.BlockSpec((GATE_W, D_MODEL), fix),
            pl.BlockSpec((1, GATE_W), fix),
        ],
        out_specs=[
            pl.BlockSpec((None, tm, tc), tile),
            pl.BlockSpec((None, tm, tp), tile),
            pl.BlockSpec((None, tm, GATE_W), row),
        ],
        out_shape=[jax.ShapeDtypeStruct((bsz, s, CONV_WIDTH), BF16),
                   jax.ShapeDtypeStruct((bsz, s, PLAIN_WIDTH), BF16),
                   jax.ShapeDtypeStruct((bsz, s, GATE_W), F32)],
        scratch_shapes=[pltpu.VMEM((tm, D_MODEL), BF16),
                        pltpu.VMEM((ncol, SUBLANES, tc), F32),
                        pltpu.VMEM((SUBLANES + tm, tc), F32)],
        compiler_params=pltpu.CompilerParams(
            dimension_semantics=("parallel", "arbitrary", "arbitrary"),
            vmem_limit_bytes=VMEM_LIMIT),
        name="inproj",
    )(x, norm_w, scale, shift, w_conv, b_conv, conv_w, conv_b, w_plain, b_plain,
      w_gate, b_gate)


def _mlstm_kernel(qk_ref, v_ref, g_ref, h_ref, cn_scr, m_scr):
    @pl.when(pl.program_id(1) == 0)
    def _():
        cn_scr[...] = jnp.zeros_like(cn_scr)
        m_scr[...] = jnp.zeros_like(m_scr)

    n = ML_CHUNK
    nrows = qk_ref.shape[0]
    row = lax.broadcasted_iota(jnp.int32, (n, n), 0)
    col = lax.broadcasted_iota(jnp.int32, (n, n), 1)
    causal = col <= row

    a_all = _cumsum_rows(_log_sigmoid(g_ref[:, LANES:]), _lower_tri(nrows, n))
    r_all = g_ref[:, :LANES] - a_all
    r_t_all = (r_all + math.log(QK_SCALE)).T

    for ch in range(nrows // n):
        rows = slice(ch * n, (ch + 1) * n)
        a = a_all[rows, :]
        r = r_all[rows, :]
        m_prev = m_scr[...]
        mm = jnp.maximum(_cummax_rows(r), m_prev)
        w_inter = jnp.exp(m_prev - mm)
        floor = jnp.exp(-(a + mm))
        mm_last = mm[n - 1:n, :]
        wk_all = jnp.exp(r - mm_last)
        m_scr[...] = a[n - 1:n, :] + mm_last
        w_inter_q = w_inter * QK_SCALE
        r_t = r_t_all[:, rows]

        def q_of(h):
            return qk_ref[rows, h * ML_DQK:(h + 1) * ML_DQK]

        def k_of(h):
            return qk_ref[rows, ML_HEADS * ML_DQK + h * ML_DQK:ML_HEADS * ML_DQK + (h + 1) * ML_DQK]

        def v_of(h):
            return v_ref[rows, h * ML_DV:(h + 1) * ML_DV]

        ones = jnp.ones((n, LANES), BF16)
        for h in range(ML_HEADS):
            q = q_of(h)
            v_ext = jnp.concatenate([v_of(h), ones], axis=1)
            cn_prev = cn_scr[h]
            w_in = w_inter_q[:, h:h + 1]

            w_intra = jnp.exp(jnp.where(causal, r_t[h:h + 1, :] - mm[:, h:h + 1], -jnp.inf))
            sc = lax.dot_general(q, k_of(h), (((1,), (1,)), ((), ())),
                                 preferred_element_type=F32) * w_intra
            numden = (jnp.dot(sc.astype(BF16), v_ext, preferred_element_type=F32)
                      + w_in * jnp.dot(q, cn_prev.astype(BF16), preferred_element_type=F32))
            den = numden[:, ML_DV:]
            inv = 1.0 / jnp.maximum(jnp.abs(den), floor[:, h:h + 1])
            hval = numden[:, :ML_DV] * jnp.concatenate([inv] * (ML_DV // LANES), axis=1)
            h_ref[rows, h * ML_DV:(h + 1) * ML_DV] = hval.astype(h_ref.dtype)

            decay = w_inter[n - 1:n, h:h + 1]
            kw = k_of(h).astype(F32) * wk_all[:, h:h + 1]
            cn_scr[h] = decay * cn_prev + jnp.dot(kw.T.astype(BF16), v_ext,
                                                  preferred_element_type=F32)


def _ssd_kernel(xs_ref, bc_ref, g_ref, acoef_ref, dskip_ref, y_ref, st_scr):
    @pl.when(pl.program_id(1) == 0)
    def _():
        st_scr[...] = jnp.zeros_like(st_scr)

    ngs = SSM_GROUPS * SSM_STATE
    pair_w = 2 * SSM_HEADDIM
    nrows = xs_ref.shape[0]
    nch = nrows // CHUNK
    dt_all = _softplus(g_ref[:, :LANES])
    a_all = _cumsum_rows(dt_all * acoef_ref[...], _lower_tri(nrows, CHUNK))

    def doubled_t(x):
        x3 = x.reshape(nch, CHUNK, LANES)
        return jnp.concatenate([x3, x3], axis=1).reshape(2 * nrows, LANES).T

    a2_t_all = doubled_t(a_all)
    dt2_t_all = doubled_t(dt_all)
    lane = lax.broadcasted_iota(jnp.int32, (1, pair_w), 1)
    left = lane < SSM_HEADDIM
    left_bf = left.astype(BF16)
    right_bf = (lane >= SSM_HEADDIM).astype(BF16)
    row = lax.broadcasted_iota(jnp.int32, (CHUNK, 2 * CHUNK), 0)
    col = lax.broadcasted_iota(jnp.int32, (CHUNK, 2 * CHUNK), 1)
    causal2 = jnp.bitwise_and(col, CHUNK - 1) <= row
    eye2 = jnp.bitwise_and(col, CHUNK - 1) == row

    for ch in range(nch):
        rows = slice(ch * CHUNK, (ch + 1) * CHUNK)
        dt = dt_all[rows, :]
        a = a_all[rows, :]
        a2_t = a2_t_all[:, ch * 2 * CHUNK:(ch + 1) * 2 * CHUNK]
        dt2_t = dt2_t_all[:, ch * 2 * CHUNK:(ch + 1) * 2 * CHUNK]

        for gi in range(SSM_GROUPS):
            bg = bc_ref[rows, gi * SSM_STATE:(gi + 1) * SSM_STATE]
            cg = bc_ref[rows, ngs + gi * SSM_STATE:ngs + (gi + 1) * SSM_STATE]
            b2 = jnp.concatenate([bg, bg], axis=0)
            scores2 = lax.dot_general(cg, b2, (((1,), (1,)), ((), ())),
                                      preferred_element_type=F32)
            st = st_scr[gi]
            y_inter = jnp.dot(cg, st.astype(BF16), preferred_element_type=F32)
            bg_t = bg.astype(F32).T.astype(BF16)
            for p in range(HEADS_PER_GROUP // 2):
                pair = gi * (HEADS_PER_GROUP // 2) + p
                la = GATE_DT + 2 * pair
                lb = la + 1
                slab = slice(pair * pair_w, (pair + 1) * pair_w)
                sl_g = slice(p * pair_w, (p + 1) * pair_w)
                a_col2 = jnp.where(left, a[:, la:la + 1], a[:, lb:lb + 1])
                a_row2 = jnp.where(left, a2_t[la:la + 1, :], a2_t[lb:lb + 1, :])
                dt_row2 = jnp.where(left, dt2_t[la:la + 1, :], dt2_t[lb:lb + 1, :])
                a_last2 = a_col2[CHUNK - 1:CHUNK, :]

                decay2 = jnp.exp(jnp.where(causal2, a_col2 - a_row2, -jnp.inf))
                mix2 = (decay2 * scores2 * dt_row2).astype(BF16)
                wts_row2 = jnp.exp(a_last2 - a_row2) * dt_row2
                diag2 = jnp.where(eye2, wts_row2, 0.0).astype(BF16)
                xp_bf = xs_ref[rows, slab]
                xp_bd = jnp.concatenate([xp_bf * left_bf, xp_bf * right_bf], axis=0)
                both = jnp.dot(jnp.concatenate([mix2, diag2], axis=0), xp_bd,
                               preferred_element_type=F32)
                y = (both[:CHUNK, :]
                     + jnp.exp(a_col2) * y_inter[:, sl_g]
                     + dskip_ref[:, slab] * xp_bf.astype(F32))
                y_ref[rows, slab] = y.astype(y_ref.dtype)

                xw = both[CHUNK:, :].astype(BF16)
                st_scr[gi, :, sl_g] = (jnp.exp(a_last2) * st[:, sl_g]
                                       + jnp.dot(bg_t, xw, preferred_element_type=F32))


def _scans_kernel(qk_ref, v_ref, g_ref, xs_ref, bc_ref, acoef_ref, dskip_ref, h_ref, y_ref,
                  cn_scr, m_scr, st_scr):
    _mlstm_kernel(qk_ref, v_ref, g_ref, h_ref, cn_scr, m_scr)
    _ssd_kernel(xs_ref, bc_ref, g_ref, acoef_ref, dskip_ref, y_ref, st_scr)


def _scans(conv_out, plain_out, gates, acoef_row, dskip_row, *, rows):
    b, s, _ = conv_out.shape
    blk = lambda bi, c: (bi, c, 0)
    return pl.pallas_call(
        _scans_kernel,
        grid=(b, s // rows),
        in_specs=[
            pl.BlockSpec((None, rows, SEG_W), lambda bi, c: (bi, c, CONV_QK)),
            pl.BlockSpec((None, rows, SEG_W), lambda bi, c: (bi, c, PLAIN_V)),
            pl.BlockSpec((None, rows, GATE_W), blk),
            pl.BlockSpec((None, rows, SEG_W), lambda bi, c: (bi, c, CONV_XS)),
            pl.BlockSpec((None, rows, SSM_BC), lambda bi, c: (bi, c, CONV_BC_BLOCK)),
            pl.BlockSpec((1, LANES), lambda bi, c: (0, 0)),
            pl.BlockSpec((1, SSM_INNER), lambda bi, c: (0, 0)),
        ],
        out_specs=[pl.BlockSpec((None, rows, ML_INNER), blk),
                   pl.BlockSpec((None, rows, SSM_INNER), blk)],
        out_shape=[jax.ShapeDtypeStruct((b, s, ML_INNER), BF16),
                   jax.ShapeDtypeStruct((b, s, SSM_INNER), BF16)],
        scratch_shapes=[pltpu.VMEM((ML_HEADS, ML_DQK, ML_DV + LANES), F32),
                        pltpu.VMEM((1, LANES), F32),
                        pltpu.VMEM((SSM_GROUPS, SSM_STATE, HEADS_PER_GROUP * SSM_HEADDIM), F32)],
        compiler_params=pltpu.CompilerParams(dimension_semantics=("parallel", "arbitrary"),
                                             vmem_limit_bytes=VMEM_LIMIT),
        name="scans",
    )(conv_out, plain_out, gates, conv_out, conv_out, acoef_row, dskip_row)


def _outproj_kernel(h_ref, o_ref, zm_ref, ys_ref, zs_ref, mg_ref, x_ref, gate_ref,
                    wm_ref, ws_ref, wo_ref, fw_ref, out_ref):
    ym = []
    for hd in range(ML_HEADS):
        sl = slice(hd * ML_DV, (hd + 1) * ML_DV)
        hh = h_ref[:, sl].astype(F32)
        rs = lax.rsqrt(jnp.mean(hh * hh, axis=-1, keepdims=True) + EPS)
        zh = zm_ref[:, sl].astype(F32)
        gates = (jnp.tanh(o_ref[:, sl].astype(F32)) + 1.0) * (jnp.tanh(zh) + 1.0)
        ym.append((gates * (zh * hh) * rs).astype(BF16))
    pm = jnp.dot(jnp.concatenate(ym, axis=1), wm_ref[...], preferred_element_type=F32)
    yn = []
    gw = SSM_INNER // SSM_GROUPS
    for gi in range(SSM_GROUPS):
        sl = slice(gi * gw, (gi + 1) * gw)
        zh = zs_ref[:, sl].astype(F32)
        t = ys_ref[:, sl].astype(F32) * zh * (jnp.tanh(zh) + 1.0)
        rs = lax.rsqrt(jnp.mean(t * t, axis=-1, keepdims=True) + EPS)
        yn.append((t * rs).astype(BF16))
    ps = jnp.dot(jnp.concatenate(yn, axis=1), ws_ref[...], preferred_element_type=F32)
    gm = jnp.tanh(mg_ref[:, :D_MODEL].astype(F32)) + 1.0
    gs = jnp.tanh(mg_ref[:, D_MODEL:].astype(F32)) + 1.0
    merged = (gm * pm + gs * ps).astype(BF16)
    r = jnp.dot(merged, wo_ref[...], preferred_element_type=F32)
    xo = x_ref[...] + gate_ref[...] * r
    ms = jnp.mean(xo * xo, axis=-1, keepdims=True)
    out_ref[...] = xo * lax.rsqrt(ms + EPS) * fw_ref[...]


def _outproj(h_m, y_s, plain_out, x, gate, w_m, w_s, w_o, final_w, *, tm):
    b, s, _ = x.shape

    def seg(k):
        return pl.BlockSpec((None, tm, SEG_W), lambda bi, i: (bi, i, k))

    def full(shape):
        return pl.BlockSpec(shape, lambda bi, i: (0,) * len(shape), pipeline_mode=pl.Buffered(1))

    return pl.pallas_call(
        _outproj_kernel,
        grid=(b, s // tm),
        in_specs=[
            pl.BlockSpec((None, tm, ML_INNER), lambda bi, i: (bi, i, 0)),
            seg(PLAIN_O), seg(PLAIN_ZM),
            pl.BlockSpec((None, tm, SSM_INNER), lambda bi, i: (bi, i, 0)),
            seg(PLAIN_ZS), seg(PLAIN_MERGE),
            pl.BlockSpec((None, tm, D_MODEL), lambda bi, i: (bi, i, 0)),
            pl.BlockSpec((None, 1, D_MODEL), lambda bi, i: (bi, 0, 0)),
            full((ML_INNER, D_MODEL)), full((SSM_INNER, D_MODEL)), full((D_MODEL, D_MODEL)),
            full((1, D_MODEL)),
        ],
        out_specs=pl.BlockSpec((None, tm, D_MODEL), lambda bi, i: (bi, i, 0)),
        out_shape=jax.ShapeDtypeStruct((b, s, D_MODEL), F32),
        compiler_params=pltpu.CompilerParams(dimension_semantics=("parallel", "parallel"),
                                             vmem_limit_bytes=VMEM_LIMIT),
        name="outproj",
    )(h_m, plain_out, plain_out, y_s, plain_out, plain_out, x, gate, w_m, w_s, w_o, final_w)


def _seg(w, k):
    return w[..., SEG_OFF[k]:SEG_OFF[k] + SEG_SIZES[k]]


def _gate_cols(i_cols, f_cols, dt_cols):
    lead = i_cols.shape[:-1]

    def z(n):
        return jnp.zeros(lead + (n,), i_cols.dtype)

    return jnp.concatenate(
        [i_cols, z(GATE_DT - ML_HEADS), dt_cols, z(LANES - GATE_DT - SSM_HEADS),
         f_cols, z(LANES - ML_HEADS)], axis=-1)


def _layer(x, c_pad, norm_w, ada_w, ada_b, w_in, b_in, ml_conv_w, ml_conv_b, ml_norm_w,
           ssm_conv_w, ssm_conv_b, ssm_a_log, ssm_d, ssm_norm_w, w_proj_m, w_proj_s, w_out,
           final_w):
    b, s, _ = x.shape
    mod = _mod(c_pad, ada_w, ada_b.reshape(1, -1))[:b]
    shift = mod[:, :D_MODEL].reshape(b, 1, D_MODEL)
    scale = mod[:, D_MODEL:2 * D_MODEL].reshape(b, 1, D_MODEL)
    gate = mod[:, 2 * D_MODEL:].reshape(b, 1, D_MODEL)
    norm_w = norm_w.reshape(1, D_MODEL)

    w_t = w_in.T

    def seg_t(k):
        return w_t[SEG_OFF[k]:SEG_OFF[k] + SEG_SIZES[k], :]

    w_conv = jnp.concatenate([seg_t(0).astype(BF16), seg_t(6).astype(BF16)], axis=0).T
    b_conv = jnp.concatenate([_seg(b_in, 0), _seg(b_in, 6)]).reshape(1, CONV_WIDTH)
    conv_w = jnp.concatenate([ml_conv_w, ssm_conv_w], axis=1)
    conv_b = jnp.concatenate([ml_conv_b, ssm_conv_b]).reshape(1, CONV_WIDTH)
    conv_b_half = 0.5 * (conv_b + b_conv * jnp.sum(conv_w, axis=0, keepdims=True))
    conv_w_half = 0.5 * conv_w
    w_plain = jnp.concatenate(
        [seg_t(1).astype(BF16)] + [(0.5 * seg_t(k)).astype(BF16) for k in (2, 3, 7, 9)], axis=0).T
    b_plain = jnp.concatenate([_seg(b_in, 1), 0.5 * _seg(b_in, 2), 0.5 * _seg(b_in, 3),
                               0.5 * _seg(b_in, 7), 0.5 * _seg(b_in, 9)]).reshape(1, PLAIN_WIDTH)
    w_gate = _gate_cols(_seg(w_in, 4), _seg(w_in, 5), _seg(w_in, 8)).T
    b_gate = _gate_cols(_seg(b_in, 4), _seg(b_in, 5), _seg(b_in, 8)).reshape(1, GATE_W)

    conv_out, plain_out, gates = _inproj(
        x, norm_w, scale, shift, w_conv, b_conv, conv_w_half, conv_b_half,
        w_plain, b_plain, w_gate, b_gate, tm=min(1024, s), ncol=5)

    acoef_row = jnp.pad(-jnp.exp(ssm_a_log.astype(F32)),
                        (GATE_DT, LANES - GATE_DT - SSM_HEADS)).reshape(1, LANES)
    dskip_row = jnp.repeat(ssm_d, SSM_HEADDIM).reshape(1, SSM_INNER)
    h_m, y_s = _scans(conv_out, plain_out, gates, acoef_row, dskip_row, rows=min(SCAN_ROWS, s))

    w_m = ((0.5 * ml_norm_w)[:, None] * w_proj_m).astype(BF16)
    w_s = (ssm_norm_w[:, None] * w_proj_s).astype(BF16)
    w_o = (0.5 * w_out).astype(BF16)
    return _outproj(h_m, y_s, plain_out, x, gate, w_m, w_s, w_o, final_w.reshape(1, D_MODEL),
                    tm=min(512, s))


def kernel(x, c, norm_w, ada_w, ada_b, w_in, b_in, ml_conv_w, ml_conv_b, ml_norm_w,
           ssm_conv_w, ssm_conv_b, ssm_a_log, ssm_d, ssm_norm_w, w_proj_m, w_proj_s, w_out,
           final_w):
    b = x.shape[0]
    assert norm_w.shape[0] == 1, "the residual + final-norm epilogue is fused for a single layer"
    c_pad = jnp.pad(c, ((0, (-b) % SUBLANES), (0, 0)))
    return _layer(x, c_pad, norm_w[0], ada_w[0], ada_b[0], w_in[0], b_in[0], ml_conv_w[0],
                  ml_conv_b[0], ml_norm_w[0], ssm_conv_w[0], ssm_conv_b[0], ssm_a_log[0],
                  ssm_d[0], ssm_norm_w[0], w_proj_m[0], w_proj_s[0], w_out[0], final_w)
```

```python
import functools
import math

import jax
import jax.numpy as jnp
from jax import lax
from jax.experimental import pallas as pl
from jax.experimental.pallas import tpu as pltpu

F32 = jnp.float32
BF16 = jnp.bfloat16

D_MODEL = 1024
CHUNK = 64
ML_CHUNK = 128
SCAN_ROWS = 512
CONV_K = 4
EPS = 1e-6
ML_HEADS = 8
ML_INNER = 2 * D_MODEL
ML_DV = ML_INNER // ML_HEADS
ML_DQK = ML_DV // 2
ML_QK2 = 2 * ML_HEADS * ML_DQK
QK_SCALE = ML_DQK ** -0.5
LOG2E = math.log2(math.e)
SSM_INNER = 2 * D_MODEL
SSM_HEADDIM = 64
SSM_HEADS = SSM_INNER // SSM_HEADDIM
SSM_GROUPS = 4
SSM_STATE = 128
SSM_BC = 2 * SSM_GROUPS * SSM_STATE
SSM_XBC = SSM_INNER + SSM_BC
HEADS_PER_GROUP = SSM_HEADS // SSM_GROUPS

SEG_SIZES = (ML_QK2, ML_INNER, ML_INNER, ML_INNER, ML_HEADS, ML_HEADS,
             SSM_XBC, SSM_INNER, SSM_HEADS, 2 * D_MODEL)
SEG_OFF = tuple(sum(SEG_SIZES[:i]) for i in range(len(SEG_SIZES)))

SEG_W = 2048
CONV_QK, CONV_XS = 0, 1
CONV_BC_BLOCK = 2 * SEG_W // SSM_BC
CONV_WIDTH = 2 * SEG_W + SSM_BC
PLAIN_V, PLAIN_O, PLAIN_ZM, PLAIN_ZS, PLAIN_MERGE = range(5)
PLAIN_WIDTH = 5 * SEG_W

LANES = 128
GATE_W = 2 * LANES
GATE_DT = 16

SUBLANES = 8
MXU_WIDTH = 256
INPROJ_PIECE = MXU_WIDTH
VMEM_LIMIT = 48 * 1024 * 1024


def _sigmoid(x):
    return 0.5 * jnp.tanh(0.5 * x) + 0.5


def _silu(x):
    return x * _sigmoid(x)


def _softplus(x):
    return jnp.maximum(x, 0.0) + jnp.log1p(jnp.exp(-jnp.abs(x)))


def _log_sigmoid(x):
    return jnp.minimum(x, 0.0) - jnp.log1p(jnp.exp(-jnp.abs(x)))


def _split2(x):
    hi = x.astype(BF16)
    lo = (x - hi.astype(F32)).astype(BF16)
    return hi, lo


def _split3(x):
    hi = x.astype(BF16)
    r1 = x - hi.astype(F32)
    mid = r1.astype(BF16)
    lo = (r1 - mid.astype(F32)).astype(BF16)
    return hi, mid, lo


def _dot_nt(a, b):
    return lax.dot_general(a, b, (((1,), (1,)), ((), ())), preferred_element_type=F32)


def _dot3(a, b, dot=functools.partial(jnp.dot, preferred_element_type=F32)):
    a_hi, a_lo = _split2(a)
    b_hi, b_lo = _split2(b)
    return dot(a_hi, b_hi) + dot(a_hi, b_lo) + dot(a_lo, b_hi)


def _cumsum_rows(x, tri):
    hi, mid, lo = _split3(x)
    return (jnp.dot(tri, hi, preferred_element_type=F32)
            + jnp.dot(tri, mid, preferred_element_type=F32)
            + jnp.dot(tri, lo, preferred_element_type=F32))


def _shift_rows(x, d, fill):
    n, w = x.shape
    xe = jnp.concatenate([jnp.full((SUBLANES, w), fill, x.dtype), x], axis=0)
    return xe[SUBLANES - d:SUBLANES - d + n, :]


def _cummax_rows(x):
    n, w = x.shape
    d = 1
    while d < n:
        if d < SUBLANES:
            sh = _shift_rows(x, d, -jnp.inf)
        else:
            sh = jnp.concatenate([jnp.full((d, w), -jnp.inf, x.dtype), x[:n - d, :]], axis=0)
        x = jnp.maximum(x, sh)
        d *= 2
    return x


def _lower_tri(n, block=None):
    r = lax.broadcasted_iota(jnp.int32, (n, n), 0)
    c = lax.broadcasted_iota(jnp.int32, (n, n), 1)
    keep = c <= r
    if block is not None and block < n:
        keep = keep & (c >= (r // block) * block)
    return keep.astype(BF16)


def _adaln(x_ref, nw_ref, sc_ref, sh_ref):
    xf = x_ref[...]
    ms = jnp.mean(xf * xf, axis=-1, keepdims=True)
    y = xf * lax.rsqrt(ms + EPS) * nw_ref[...]
    return y * (1.0 + sc_ref[...]) + sh_ref[...]


def _mod_kernel(c_ref, w_ref, b_ref, o_ref):
    o_ref[...] = _dot3(_silu(c_ref[...]), w_ref[...]) + b_ref[...]


def _mod(c_pad, ada_w, ada_b):
    rows = c_pad.shape[0]
    n = ada_w.shape[1]
    tn = D_MODEL
    return pl.pallas_call(
        _mod_kernel,
        grid=(n // tn,),
        in_specs=[pl.BlockSpec((rows, D_MODEL), lambda j: (0, 0)),
                  pl.BlockSpec((D_MODEL, tn), lambda j: (0, j)),
                  pl.BlockSpec((1, tn), lambda j: (0, j))],
        out_specs=pl.BlockSpec((rows, tn), lambda j: (0, j)),
        out_shape=jax.ShapeDtypeStruct((rows, n), F32),
        compiler_params=pltpu.CompilerParams(dimension_semantics=("arbitrary",),
                                             vmem_limit_bytes=VMEM_LIMIT),
        name="mod",
    )(c_pad, ada_w, ada_b)


def _shift_down(xe, s):
    n = xe.shape[0] - SUBLANES
    w = xe.shape[1]
    sub = lax.broadcasted_iota(jnp.int32, (n, w), 0) % SUBLANES
    mixed = jnp.where(sub >= SUBLANES - s, xe[:n, :], xe[SUBLANES:, :])
    return pltpu.roll(mixed.reshape(-1, SUBLANES, w), s, axis=1).reshape(n, w)


def _inproj_kernel(x_ref, nw_ref, sc_ref, sh_ref, wc_ref, bc_ref, cw_ref, cb_ref,
                   wp_ref, bp_ref, wg_ref, bg_ref, oc_ref, op_ref, og_ref, u_ref, tail_ref,
                   acc_ref):
    i = pl.program_id(1)
    j = pl.program_id(2)
    tm = x_ref.shape[0]

    @pl.when(j == 0)
    def _():
        u = _adaln(x_ref, nw_ref, sc_ref, sh_ref)
        u_ref[...] = u.astype(BF16)
        og_ref[...] = _dot3(u, wg_ref[...], _dot_nt) + bg_ref[...]

    @pl.when(i == 0)
    def _():
        tail_ref[j] = jnp.broadcast_to(-bc_ref[...], tail_ref.shape[1:])

    pieces = wc_ref.shape[1] // INPROJ_PIECE
    pw = wp_ref.shape[1] // pieces

    def matmuls(p):
        cs = slice(p * INPROJ_PIECE, (p + 1) * INPROJ_PIECE)
        ps = slice(p * pw, (p + 1) * pw)
        u = u_ref[...]
        acc_ref[:SUBLANES, cs] = tail_ref[j, :, cs]
        acc_ref[SUBLANES:, cs] = jnp.dot(u, wc_ref[:, cs], preferred_element_type=F32)
        op_ref[:, ps] = (jnp.dot(u, wp_ref[:, ps], preferred_element_type=F32)
                         + bp_ref[:, ps]).astype(op_ref.dtype)

    def epilogue(p):
        cs = slice(p * INPROJ_PIECE, (p + 1) * INPROJ_PIECE)
        xe = acc_ref[:, cs]
        acc = xe[SUBLANES:, :]
        tail_ref[j, :, cs] = xe[tm:, :]
        yh = cb_ref[:, cs] + acc * cw_ref[CONV_K - 1:CONV_K, cs]
        for k in range(CONV_K - 1):
            yh = yh + _shift_down(xe, CONV_K - 1 - k) * cw_ref[k:k + 1, cs]
        oc_ref[:, cs] = (yh * (jnp.tanh(yh) + 1.0)).astype(oc_ref.dtype)

    for p in range(pieces + 1):
        if p < pieces:
            matmuls(p)
        if p > 0:
            epilogue(p - 1)


def _inproj(x, norm_w, scale, shift, w_conv, b_conv, conv_w, conv_b, w_plain, b_plain,
            w_gate, b_gate, *, tm, ncol):
    bsz, s, _ = x.shape
    tc = CONV_WIDTH // ncol
    tp = PLAIN_WIDTH // ncol
    row = lambda bi, i, j: (bi, i, 0)
    col = lambda bi, i, j: (0, j)
    fix = lambda bi, i, j: (0, 0)
    per_b = lambda bi, i, j: (bi, 0, 0)
    tile = lambda bi, i, j: (bi, i, j)
    return pl.pallas_call(
        _inproj_kernel,
        grid=(bsz, s // tm, ncol),
        in_specs=[
            pl.BlockSpec((None, tm, D_MODEL), row),
            pl.BlockSpec((1, D_MODEL), fix),
            pl.BlockSpec((None, 1, D_MODEL), per_b),
            pl.BlockSpec((None, 1, D_MODEL), per_b),
            pl.BlockSpec((D_MODEL, tc), col),
            pl.BlockSpec((1, tc), col),
            pl.BlockSpec((CONV_K, tc), col),
            pl.BlockSpec((1, tc), col),
            pl.BlockSpec((D_MODEL, tp), col),
            pl.BlockSpec((1, tp), col),
            pl.BlockSpec((GATE_W, D_MODEL), fix),
            pl.BlockSpec((1, GATE_W), fix),
        ],
        out_specs=[
            pl.BlockSpec((None, tm, tc), tile),
            pl.BlockSpec((None, tm, tp), tile),
            pl.BlockSpec((None, tm, GATE_W), row),
        ],
        out_shape=[jax.ShapeDtypeStruct((bsz, s, CONV_WIDTH), BF16),
                   jax.ShapeDtypeStruct((bsz, s, PLAIN_WIDTH), BF16),
                   jax.ShapeDtypeStruct((bsz, s, GATE_W), F32)],
        scratch_shapes=[pltpu.VMEM((tm, D_MODEL), BF16),
                        pltpu.VMEM((ncol, SUBLANES, tc), F32),
                        pltpu.VMEM((SUBLANES + tm, tc), F32)],
        compiler_params=pltpu.CompilerParams(
            dimension_semantics=("parallel", "arbitrary", "arbitrary"),
            vmem_limit_bytes=VMEM_LIMIT),
        name="inproj",
    )(x, norm_w, scale, shift, w_conv, b_conv, conv_w, conv_b, w_plain, b_plain,
      w_gate, b_gate)


def _mlstm_kernel(qk_ref, v_ref, g_ref, h_ref, cn_scr, m_scr):
    @pl.when(pl.program_id(1) == 0)
    def _():
        cn_scr[...] = jnp.zeros_like(cn_scr)
        m_scr[...] = jnp.zeros_like(m_scr)

    n = ML_CHUNK
    nrows = qk_ref.shape[0]
    row = lax.broadcasted_iota(jnp.int32, (n, n), 0)
    col = lax.broadcasted_iota(jnp.int32, (n, n), 1)
    causal = col <= row

    a_all = _cumsum_rows(_log_sigmoid(g_ref[:, LANES:]), _lower_tri(nrows, n))
    r_all = g_ref[:, :LANES] - a_all
    r2_t_all = ((r_all + math.log(QK_SCALE)) * LOG2E).T

    for ch in range(nrows // n):
        rows = slice(ch * n, (ch + 1) * n)
        a = a_all[rows, :]
        r = r_all[rows, :]
        m_prev = m_scr[...]
        mm = jnp.maximum(_cummax_rows(r), m_prev)
        w_inter = jnp.exp(m_prev - mm)
        floor = jnp.exp(-(a + mm))
        mm_last = mm[n - 1:n, :]
        wk_all = jnp.exp(r - mm_last)
        m_scr[...] = a[n - 1:n, :] + mm_last
        w_inter_q = w_inter * QK_SCALE
        mm2 = mm * LOG2E
        r2_t = r2_t_all[:, rows]

        def q_of(h):
            return qk_ref[rows, h * ML_DQK:(h + 1) * ML_DQK]

        def k_of(h):
            return qk_ref[rows, ML_HEADS * ML_DQK + h * ML_DQK:ML_HEADS * ML_DQK + (h + 1) * ML_DQK]

        def v_of(h):
            return v_ref[rows, h * ML_DV:(h + 1) * ML_DV]

        ones = jnp.ones((n, LANES), BF16)
        for h in range(ML_HEADS):
            q = q_of(h)
            v_ext = jnp.concatenate([v_of(h), ones], axis=1)
            cn_prev = cn_scr[h]
            w_in = w_inter_q[:, h:h + 1]

            w_intra = jnp.exp2(jnp.where(causal, r2_t[h:h + 1, :] - mm2[:, h:h + 1], -jnp.inf))
            sc = lax.dot_general(q, k_of(h), (((1,), (1,)), ((), ())),
                                 preferred_element_type=F32) * w_intra
            numden = (jnp.dot(sc.astype(BF16), v_ext, preferred_element_type=F32)
                      + w_in * jnp.dot(q, cn_prev.astype(BF16), preferred_element_type=F32))
            den = numden[:, ML_DV:]
            inv = 1.0 / jnp.maximum(jnp.abs(den), floor[:, h:h + 1])
            hval = numden[:, :ML_DV] * jnp.concatenate([inv] * (ML_DV // LANES), axis=1)
            h_ref[rows, h * ML_DV:(h + 1) * ML_DV] = hval.astype(h_ref.dtype)

            decay = w_inter[n - 1:n, h:h + 1]
            kw = k_of(h).astype(F32) * wk_all[:, h:h + 1]
            cn_scr[h] = decay * cn_prev + jnp.dot(kw.T.astype(BF16), v_ext,
                                                  preferred_element_type=F32)


def _ssd_kernel(xs_ref, bc_ref, g_ref, acoef_ref, dskip_ref, y_ref, st_scr):
    @pl.when(pl.program_id(1) == 0)
    def _():
        st_scr[...] = jnp.zeros_like(st_scr)

    ngs = SSM_GROUPS * SSM_STATE
    pair_w = 2 * SSM_HEADDIM
    nrows = xs_ref.shape[0]
    nch = nrows // CHUNK
    dt_all = _softplus(g_ref[:, :LANES])
    a_all = _cumsum_rows(dt_all * acoef_ref[...], _lower_tri(nrows, CHUNK)) * LOG2E
    src_all = a_all - jnp.log2(dt_all)

    def doubled_t(x):
        x3 = x.reshape(nch, CHUNK, LANES)
        return jnp.concatenate([x3, x3], axis=1).reshape(2 * nrows, LANES).T

    src2_t_all = doubled_t(src_all)
    lane = lax.broadcasted_iota(jnp.int32, (1, pair_w), 1)
    left = lane < SSM_HEADDIM
    left_bf = left.astype(BF16)
    right_bf = (lane >= SSM_HEADDIM).astype(BF16)
    row = lax.broadcasted_iota(jnp.int32, (CHUNK, 2 * CHUNK), 0)
    col = lax.broadcasted_iota(jnp.int32, (CHUNK, 2 * CHUNK), 1)
    causal2 = jnp.bitwise_and(col, CHUNK - 1) <= row
    eye2 = jnp.bitwise_and(col, CHUNK - 1) == row

    for ch in range(nch):
        rows = slice(ch * CHUNK, (ch + 1) * CHUNK)
        a = a_all[rows, :]
        src2_t = src2_t_all[:, ch * 2 * CHUNK:(ch + 1) * 2 * CHUNK]

        for gi in range(SSM_GROUPS):
            bg = bc_ref[rows, gi * SSM_STATE:(gi + 1) * SSM_STATE]
            cg = bc_ref[rows, ngs + gi * SSM_STATE:ngs + (gi + 1) * SSM_STATE]
            b2 = jnp.concatenate([bg, bg], axis=0)
            scores2 = lax.dot_general(cg, b2, (((1,), (1,)), ((), ())),
                                      preferred_element_type=F32)
            st = st_scr[gi]
            y_inter = jnp.dot(cg, st.astype(BF16), preferred_element_type=F32)
            bg_t = bg.astype(F32).T.astype(BF16)
            for p in range(HEADS_PER_GROUP // 2):
                pair = gi * (HEADS_PER_GROUP // 2) + p
                la = GATE_DT + 2 * pair
                lb = la + 1
                slab = slice(pair * pair_w, (pair + 1) * pair_w)
                sl_g = slice(p * pair_w, (p + 1) * pair_w)
                a_col2 = jnp.where(left, a[:, la:la + 1], a[:, lb:lb + 1])
                src_row2 = jnp.where(left, src2_t[la:la + 1, :], src2_t[lb:lb + 1, :])
                a_last2 = a_col2[CHUNK - 1:CHUNK, :]

                decay_dt2 = jnp.exp2(jnp.where(causal2, a_col2 - src_row2, -jnp.inf))
                mix2 = (decay_dt2 * scores2).astype(BF16)
                wts_row2 = jnp.exp2(a_last2 - src_row2)
                diag2 = jnp.where(eye2, wts_row2, 0.0).astype(BF16)
                xp_bf = xs_ref[rows, slab]
                xp_bd = jnp.concatenate([xp_bf * left_bf, xp_bf * right_bf], axis=0)
                both = jnp.dot(jnp.concatenate([mix2, diag2], axis=0), xp_bd,
                               preferred_element_type=F32)
                y = (both[:CHUNK, :]
                     + jnp.exp2(a_col2) * y_inter[:, sl_g]
                     + dskip_ref[:, slab] * xp_bf.astype(F32))
                y_ref[rows, slab] = y.astype(y_ref.dtype)

                xw = both[CHUNK:, :].astype(BF16)
                st_scr[gi, :, sl_g] = (jnp.exp2(a_last2) * st[:, sl_g]
                                       + jnp.dot(bg_t, xw, preferred_element_type=F32))


def _scans_kernel(qk_ref, v_ref, g_ref, xs_ref, bc_ref, acoef_ref, dskip_ref, h_ref, y_ref,
                  cn_scr, m_scr, st_scr):
    _mlstm_kernel(qk_ref, v_ref, g_ref, h_ref, cn_scr, m_scr)
    _ssd_kernel(xs_ref, bc_ref, g_ref, acoef_ref, dskip_ref, y_ref, st_scr)


def _scans(conv_out, plain_out, gates, acoef_row, dskip_row, *, rows):
    b, s, _ = conv_out.shape
    blk = lambda bi, c: (bi, c, 0)
    return pl.pallas_call(
        _scans_kernel,
        grid=(b, s // rows),
        in_specs=[
            pl.BlockSpec((None, rows, SEG_W), lambda bi, c: (bi, c, CONV_QK)),
            pl.BlockSpec((None, rows, SEG_W), lambda bi, c: (bi, c, PLAIN_V)),
            pl.BlockSpec((None, rows, GATE_W), blk),
            pl.BlockSpec((None, rows, SEG_W), lambda bi, c: (bi, c, CONV_XS)),
            pl.BlockSpec((None, rows, SSM_BC), lambda bi, c: (bi, c, CONV_BC_BLOCK)),
            pl.BlockSpec((1, LANES), lambda bi, c: (0, 0)),
            pl.BlockSpec((1, SSM_INNER), lambda bi, c: (0, 0)),
        ],
        out_specs=[pl.BlockSpec((None, rows, ML_INNER), blk),
                   pl.BlockSpec((None, rows, SSM_INNER), blk)],
        out_shape=[jax.ShapeDtypeStruct((b, s, ML_INNER), BF16),
                   jax.ShapeDtypeStruct((b, s, SSM_INNER), BF16)],
        scratch_shapes=[pltpu.VMEM((ML_HEADS, ML_DQK, ML_DV + LANES), F32),
                        pltpu.VMEM((1, LANES), F32),
                        pltpu.VMEM((SSM_GROUPS, SSM_STATE, HEADS_PER_GROUP * SSM_HEADDIM), F32)],
        compiler_params=pltpu.CompilerParams(dimension_semantics=("parallel", "arbitrary"),
                                             vmem_limit_bytes=VMEM_LIMIT),
        name="scans",
    )(conv_out, plain_out, gates, conv_out, conv_out, acoef_row, dskip_row)


def _outproj_kernel(h_ref, o_ref, zm_ref, ys_ref, zs_ref, mg_ref, x_ref, gate_ref,
                    wm_ref, ws_ref, wo_ref, fw_ref, out_ref):
    ym = []
    for hd in range(ML_HEADS):
        sl = slice(hd * ML_DV, (hd + 1) * ML_DV)
        hh = h_ref[:, sl].astype(F32)
        rs = lax.rsqrt(jnp.mean(hh * hh, axis=-1, keepdims=True) + EPS)
        zh = zm_ref[:, sl].astype(F32)
        gates = (jnp.tanh(o_ref[:, sl].astype(F32)) + 1.0) * (jnp.tanh(zh) + 1.0)
        ym.append((gates * (zh * hh) * rs).astype(BF16))
    pm = jnp.dot(jnp.concatenate(ym, axis=1), wm_ref[...], preferred_element_type=F32)
    yn = []
    gw = SSM_INNER // SSM_GROUPS
    for gi in range(SSM_GROUPS):
        sl = slice(gi * gw, (gi + 1) * gw)
        zh = zs_ref[:, sl].astype(F32)
        t = ys_ref[:, sl].astype(F32) * zh * (jnp.tanh(zh) + 1.0)
        rs = lax.rsqrt(jnp.mean(t * t, axis=-1, keepdims=True) + EPS)
        yn.append((t * rs).astype(BF16))
    ps = jnp.dot(jnp.concatenate(yn, axis=1), ws_ref[...], preferred_element_type=F32)
    gm = jnp.tanh(mg_ref[:, :D_MODEL].astype(F32)) + 1.0
    gs = jnp.tanh(mg_ref[:, D_MODEL:].astype(F32)) + 1.0
    merged = (gm * pm + gs * ps).astype(BF16)
    r = jnp.dot(merged, wo_ref[...], preferred_element_type=F32)
    xo = x_ref[...] + gate_ref[...] * r
    ms = jnp.mean(xo * xo, axis=-1, keepdims=True)
    out_ref[...] = xo * lax.rsqrt(ms + EPS) * fw_ref[...]


def _outproj(h_m, y_s, plain_out, x, gate, w_m, w_s, w_o, final_w, *, tm):
    b, s, _ = x.shape

    def seg(k):
        return pl.BlockSpec((None, tm, SEG_W), lambda bi, i: (bi, i, k))

    def full(shape):
        return pl.BlockSpec(shape, lambda bi, i: (0,) * len(shape), pipeline_mode=pl.Buffered(1))

    return pl.pallas_call(
        _outproj_kernel,
        grid=(b, s // tm),
        in_specs=[
            pl.BlockSpec((None, tm, ML_INNER), lambda bi, i: (bi, i, 0)),
            seg(PLAIN_O), seg(PLAIN_ZM),
            pl.BlockSpec((None, tm, SSM_INNER), lambda bi, i: (bi, i, 0)),
            seg(PLAIN_ZS), seg(PLAIN_MERGE),
            pl.BlockSpec((None, tm, D_MODEL), lambda bi, i: (bi, i, 0)),
            pl.BlockSpec((None, 1, D_MODEL), lambda bi, i: (bi, 0, 0)),
            full((ML_INNER, D_MODEL)), full((SSM_INNER, D_MODEL)), full((D_MODEL, D_MODEL)),
            full((1, D_MODEL)),
        ],
        out_specs=pl.BlockSpec((None, tm, D_MODEL), lambda bi, i: (bi, i, 0)),
        out_shape=jax.ShapeDtypeStruct((b, s, D_MODEL), F32),
        compiler_params=pltpu.CompilerParams(dimension_semantics=("parallel", "parallel"),
                                             vmem_limit_bytes=VMEM_LIMIT),
        name="outproj",
    )(h_m, plain_out, plain_out, y_s, plain_out, plain_out, x, gate, w_m, w_s, w_o, final_w)


def _seg(w, k):
    return w[..., SEG_OFF[k]:SEG_OFF[k] + SEG_SIZES[k]]


def _gate_cols(i_cols, f_cols, dt_cols):
    lead = i_cols.shape[:-1]

    def z(n):
        return jnp.zeros(lead + (n,), i_cols.dtype)

    return jnp.concatenate(
        [i_cols, z(GATE_DT - ML_HEADS), dt_cols, z(LANES - GATE_DT - SSM_HEADS),
         f_cols, z(LANES - ML_HEADS)], axis=-1)


def _layer(x, c_pad, norm_w, ada_w, ada_b, w_in, b_in, ml_conv_w, ml_conv_b, ml_norm_w,
           ssm_conv_w, ssm_conv_b, ssm_a_log, ssm_d, ssm_norm_w, w_proj_m, w_proj_s, w_out,
           final_w):
    b, s, _ = x.shape
    mod = _mod(c_pad, ada_w, ada_b.reshape(1, -1))[:b]
    shift = mod[:, :D_MODEL].reshape(b, 1, D_MODEL)
    scale = mod[:, D_MODEL:2 * D_MODEL].reshape(b, 1, D_MODEL)
    gate = mod[:, 2 * D_MODEL:].reshape(b, 1, D_MODEL)
    norm_w = norm_w.reshape(1, D_MODEL)

    w_t = w_in.T

    def seg_t(k):
        return w_t[SEG_OFF[k]:SEG_OFF[k] + SEG_SIZES[k], :]

    w_conv = jnp.concatenate([seg_t(0).astype(BF16), seg_t(6).astype(BF16)], axis=0).T
    b_conv = jnp.concatenate([_seg(b_in, 0), _seg(b_in, 6)]).reshape(1, CONV_WIDTH)
    conv_w = jnp.concatenate([ml_conv_w, ssm_conv_w], axis=1)
    conv_b = jnp.concatenate([ml_conv_b, ssm_conv_b]).reshape(1, CONV_WIDTH)
    conv_b_half = 0.5 * (conv_b + b_conv * jnp.sum(conv_w, axis=0, keepdims=True))
    conv_w_half = 0.5 * conv_w
    w_plain = jnp.concatenate(
        [seg_t(1).astype(BF16)] + [(0.5 * seg_t(k)).astype(BF16) for k in (2, 3, 7, 9)], axis=0).T
    b_plain = jnp.concatenate([_seg(b_in, 1), 0.5 * _seg(b_in, 2), 0.5 * _seg(b_in, 3),
                               0.5 * _seg(b_in, 7), 0.5 * _seg(b_in, 9)]).reshape(1, PLAIN_WIDTH)
    w_gate = _gate_cols(_seg(w_in, 4), _seg(w_in, 5), _seg(w_in, 8)).T
    b_gate = _gate_cols(_seg(b_in, 4), _seg(b_in, 5), _seg(b_in, 8)).reshape(1, GATE_W)

    conv_out, plain_out, gates = _inproj(
        x, norm_w, scale, shift, w_conv, b_conv, conv_w_half, conv_b_half,
        w_plain, b_plain, w_gate, b_gate, tm=min(1024, s), ncol=5)

    acoef_row = jnp.pad(-jnp.exp(ssm_a_log.astype(F32)),
                        (GATE_DT, LANES - GATE_DT - SSM_HEADS)).reshape(1, LANES)
    dskip_row = jnp.repeat(ssm_d, SSM_HEADDIM).reshape(1, SSM_INNER)
    h_m, y_s = _scans(conv_out, plain_out, gates, acoef_row, dskip_row, rows=min(SCAN_ROWS, s))

    w_m = ((0.5 * ml_norm_w)[:, None] * w_proj_m).astype(BF16)
    w_s = (ssm_norm_w[:, None] * w_proj_s).astype(BF16)
    w_o = (0.5 * w_out).astype(BF16)
    return _outproj(h_m, y_s, plain_out, x, gate, w_m, w_s, w_o, final_w.reshape(1, D_MODEL),
                    tm=min(512, s))


def kernel(x, c, norm_w, ada_w, ada_b, w_in, b_in, ml_conv_w, ml_conv_b, ml_norm_w,
           ssm_conv_w, ssm_conv_b, ssm_a_log, ssm_d, ssm_norm_w, w_proj_m, w_proj_s, w_out,
           final_w):
    b = x.shape[0]
    assert norm_w.shape[0] == 1, "the residual + final-norm epilogue is fused for a single layer"
    c_pad = jnp.pad(c, ((0, (-b) % SUBLANES), (0, 0)))
    return _layer(x, c_pad, norm_w[0], ada_w[0], ada_b[0], w_in[0], b_in[0], ml_conv_w[0],
                  ml_conv_b[0], ml_norm_w[0], ssm_conv_w[0], ssm_conv_b[0], ssm_a_log[0],
                  ssm_d[0], ssm_norm_w[0], w_proj_m[0], w_proj_s[0], w_out[0], final_w)
```

```python
import functools
import math

import jax
import jax.numpy as jnp
from jax import lax
from jax.experimental import pallas as pl
from jax.experimental.pallas import tpu as pltpu

F32 = jnp.float32
BF16 = jnp.bfloat16

D_MODEL = 1024
CHUNK = 64
ML_CHUNK = 256
SCAN_ROWS = 512
CONV_K = 4
EPS = 1e-6
ML_HEADS = 8
ML_INNER = 2 * D_MODEL
ML_DV = ML_INNER // ML_HEADS
ML_DQK = ML_DV // 2
ML_QK2 = 2 * ML_HEADS * ML_DQK
QK_SCALE = ML_DQK ** -0.5
LOG2E = math.log2(math.e)
SSM_INNER = 2 * D_MODEL
SSM_HEADDIM = 64
SSM_HEADS = SSM_INNER // SSM_HEADDIM
SSM_GROUPS = 4
SSM_STATE = 128
SSM_BC = 2 * SSM_GROUPS * SSM_STATE
SSM_XBC = SSM_INNER + SSM_BC
HEADS_PER_GROUP = SSM_HEADS // SSM_GROUPS

SEG_SIZES = (ML_QK2, ML_INNER, ML_INNER, ML_INNER, ML_HEADS, ML_HEADS,
             SSM_XBC, SSM_INNER, SSM_HEADS, 2 * D_MODEL)
SEG_OFF = tuple(sum(SEG_SIZES[:i]) for i in range(len(SEG_SIZES)))

SEG_W = 2048
CONV_QK, CONV_XS = 0, 1
CONV_BC_BLOCK = 2 * SEG_W // SSM_BC
CONV_WIDTH = 2 * SEG_W + SSM_BC
PLAIN_V, PLAIN_O, PLAIN_ZM, PLAIN_ZS, PLAIN_MERGE = range(5)
PLAIN_WIDTH = 5 * SEG_W

LANES = 128
GATE_W = 2 * LANES
GATE_DT = 16

SUBLANES = 8
MXU_WIDTH = 256
INPROJ_PIECE = MXU_WIDTH
VMEM_LIMIT = 48 * 1024 * 1024


def _sigmoid(x):
    return 0.5 * jnp.tanh(0.5 * x) + 0.5


def _silu(x):
    return x * _sigmoid(x)


def _softplus(x):
    return jnp.maximum(x, 0.0) + jnp.log1p(jnp.exp(-jnp.abs(x)))


def _log_sigmoid(x):
    return jnp.minimum(x, 0.0) - jnp.log1p(jnp.exp(-jnp.abs(x)))


def _split2(x):
    hi = x.astype(BF16)
    lo = (x - hi.astype(F32)).astype(BF16)
    return hi, lo


def _split3(x):
    hi = x.astype(BF16)
    r1 = x - hi.astype(F32)
    mid = r1.astype(BF16)
    lo = (r1 - mid.astype(F32)).astype(BF16)
    return hi, mid, lo


def _dot_nt(a, b):
    return lax.dot_general(a, b, (((1,), (1,)), ((), ())), preferred_element_type=F32)


def _dot3(a, b, dot=functools.partial(jnp.dot, preferred_element_type=F32)):
    a_hi, a_lo = _split2(a)
    b_hi, b_lo = _split2(b)
    return dot(a_hi, b_hi) + dot(a_hi, b_lo) + dot(a_lo, b_hi)


def _cumsum_rows(x, tri):
    hi, mid, lo = _split3(x)
    return (jnp.dot(tri, hi, preferred_element_type=F32)
            + jnp.dot(tri, mid, preferred_element_type=F32)
            + jnp.dot(tri, lo, preferred_element_type=F32))


def _shift_rows(x, d, fill):
    n, w = x.shape
    xe = jnp.concatenate([jnp.full((SUBLANES, w), fill, x.dtype), x], axis=0)
    return xe[SUBLANES - d:SUBLANES - d + n, :]


def _cummax_rows(x):
    n, w = x.shape
    d = 1
    while d < n:
        if d < SUBLANES:
            sh = _shift_rows(x, d, -jnp.inf)
        else:
            sh = jnp.concatenate([jnp.full((d, w), -jnp.inf, x.dtype), x[:n - d, :]], axis=0)
        x = jnp.maximum(x, sh)
        d *= 2
    return x


def _lower_tri(n, block=None):
    r = lax.broadcasted_iota(jnp.int32, (n, n), 0)
    c = lax.broadcasted_iota(jnp.int32, (n, n), 1)
    keep = c <= r
    if block is not None and block < n:
        keep = keep & (c >= (r // block) * block)
    return keep.astype(BF16)


def _adaln(x_ref, nw_ref, sc_ref, sh_ref):
    xf = x_ref[...]
    ms = jnp.mean(xf * xf, axis=-1, keepdims=True)
    y = xf * lax.rsqrt(ms + EPS) * nw_ref[...]
    return y * (1.0 + sc_ref[...]) + sh_ref[...]


def _mod_kernel(c_ref, w_ref, b_ref, o_ref):
    o_ref[...] = _dot3(_silu(c_ref[...]), w_ref[...]) + b_ref[...]


def _mod(c_pad, ada_w, ada_b):
    rows = c_pad.shape[0]
    n = ada_w.shape[1]
    tn = D_MODEL
    return pl.pallas_call(
        _mod_kernel,
        grid=(n // tn,),
        in_specs=[pl.BlockSpec((rows, D_MODEL), lambda j: (0, 0)),
                  pl.BlockSpec((D_MODEL, tn), lambda j: (0, j)),
                  pl.BlockSpec((1, tn), lambda j: (0, j))],
        out_specs=pl.BlockSpec((rows, tn), lambda j: (0, j)),
        out_shape=jax.ShapeDtypeStruct((rows, n), F32),
        compiler_params=pltpu.CompilerParams(dimension_semantics=("arbitrary",),
                                             vmem_limit_bytes=VMEM_LIMIT),
        name="mod",
    )(c_pad, ada_w, ada_b)


def _shift_down(xe, s):
    n = xe.shape[0] - SUBLANES
    w = xe.shape[1]
    sub = lax.broadcasted_iota(jnp.int32, (n, w), 0) % SUBLANES
    mixed = jnp.where(sub >= SUBLANES - s, xe[:n, :], xe[SUBLANES:, :])
    return pltpu.roll(mixed.reshape(-1, SUBLANES, w), s, axis=1).reshape(n, w)


def _inproj_kernel(x_ref, nw_ref, sc_ref, sh_ref, wc_ref, bc_ref, cw_ref, cb_ref,
                   wp_ref, bp_ref, wg_ref, bg_ref, oc_ref, op_ref, og_ref, u_ref, tail_ref,
                   acc_ref):
    i = pl.program_id(1)
    j = pl.program_id(2)
    tm = x_ref.shape[0]

    @pl.when(j == 0)
    def _():
        u = _adaln(x_ref, nw_ref, sc_ref, sh_ref)
        u_ref[...] = u.astype(BF16)
        og_ref[...] = _dot3(u, wg_ref[...], _dot_nt) + bg_ref[...]

    @pl.when(i == 0)
    def _():
        tail_ref[j] = jnp.broadcast_to(-bc_ref[...], tail_ref.shape[1:])

    pieces = wc_ref.shape[1] // INPROJ_PIECE
    pw = wp_ref.shape[1] // pieces

    def matmuls(p):
        cs = slice(p * INPROJ_PIECE, (p + 1) * INPROJ_PIECE)
        ps = slice(p * pw, (p + 1) * pw)
        u = u_ref[...]
        acc_ref[:SUBLANES, cs] = tail_ref[j, :, cs]
        acc_ref[SUBLANES:, cs] = jnp.dot(u, wc_ref[:, cs], preferred_element_type=F32)
        op_ref[:, ps] = (jnp.dot(u, wp_ref[:, ps], preferred_element_type=F32)
                         + bp_ref[:, ps]).astype(op_ref.dtype)

    def epilogue(p):
        cs = slice(p * INPROJ_PIECE, (p + 1) * INPROJ_PIECE)
        xe = acc_ref[:, cs]
        acc = xe[SUBLANES:, :]
        tail_ref[j, :, cs] = xe[tm:, :]
        yh = cb_ref[:, cs] + acc * cw_ref[CONV_K - 1:CONV_K, cs]
        for k in range(CONV_K - 1):
            yh = yh + _shift_down(xe, CONV_K - 1 - k) * cw_ref[k:k + 1, cs]
        oc_ref[:, cs] = (yh * (jnp.tanh(yh) + 1.0)).astype(oc_ref.dtype)

    for p in range(pieces + 1):
        if p < pieces:
            matmuls(p)
        if p > 0:
            epilogue(p - 1)


def _inproj(x, norm_w, scale, shift, w_conv, b_conv, conv_w, conv_b, w_plain, b_plain,
            w_gate, b_gate, *, tm, ncol):
    bsz, s, _ = x.shape
    tc = CONV_WIDTH // ncol
    tp = PLAIN_WIDTH // ncol
    row = lambda bi, i, j: (bi, i, 0)
    col = lambda bi, i, j: (0, j)
    fix = lambda bi, i, j: (0, 0)
    per_b = lambda bi, i, j: (bi, 0, 0)
    tile = lambda bi, i, j: (bi, i, j)
    return pl.pallas_call(
        _inproj_kernel,
        grid=(bsz, s // tm, ncol),
        in_specs=[
            pl.BlockSpec((None, tm, D_MODEL), row),
            pl.BlockSpec((1, D_MODEL), fix),
            pl.BlockSpec((None, 1, D_MODEL), per_b),
            pl.BlockSpec((None, 1, D_MODEL), per_b),
            pl.BlockSpec((D_MODEL, tc), col),
            pl.BlockSpec((1, tc), col),
            pl.BlockSpec((CONV_K, tc), col),
            pl.BlockSpec((1, tc), col),
            pl.BlockSpec((D_MODEL, tp), col),
            pl.BlockSpec((1, tp), col),
            pl.BlockSpec((GATE_W, D_MODEL), fix),
            pl.BlockSpec((1, GATE_W), fix),
        ],
        out_specs=[
            pl.BlockSpec((None, tm, tc), tile),
            pl.BlockSpec((None, tm, tp), tile),
            pl.BlockSpec((None, tm, GATE_W), row),
        ],
        out_shape=[jax.ShapeDtypeStruct((bsz, s, CONV_WIDTH), BF16),
                   jax.ShapeDtypeStruct((bsz, s, PLAIN_WIDTH), BF16),
                   jax.ShapeDtypeStruct((bsz, s, GATE_W), F32)],
        scratch_shapes=[pltpu.VMEM((tm, D_MODEL), BF16),
                        pltpu.VMEM((ncol, SUBLANES, tc), F32),
                        pltpu.VMEM((SUBLANES + tm, tc), F32)],
        compiler_params=pltpu.CompilerParams(
            dimension_semantics=("parallel", "arbitrary", "arbitrary"),
            vmem_limit_bytes=VMEM_LIMIT),
        name="inproj",
    )(x, norm_w, scale, shift, w_conv, b_conv, conv_w, conv_b, w_plain, b_plain,
      w_gate, b_gate)


def _mlstm_kernel(qk_ref, v_ref, g_ref, h_ref, cn_scr, m_scr):
    @pl.when(pl.program_id(1) == 0)
    def _():
        cn_scr[...] = jnp.zeros_like(cn_scr)
        m_scr[...] = jnp.zeros_like(m_scr)

    n = ML_CHUNK
    nrows = qk_ref.shape[0]
    row = lax.broadcasted_iota(jnp.int32, (n, n), 0)
    col = lax.broadcasted_iota(jnp.int32, (n, n), 1)
    causal = col <= row

    a_all = _cumsum_rows(_log_sigmoid(g_ref[:, LANES:]), _lower_tri(nrows, n))
    r_all = g_ref[:, :LANES] - a_all
    r2_t_all = ((r_all + math.log(QK_SCALE)) * LOG2E).T

    for ch in range(nrows // n):
        rows = slice(ch * n, (ch + 1) * n)
        a = a_all[rows, :]
        r = r_all[rows, :]
        m_prev = m_scr[...]
        mm = jnp.maximum(_cummax_rows(r), m_prev)
        w_inter = jnp.exp(m_prev - mm)
        floor = jnp.exp(-(a + mm))
        mm_last = mm[n - 1:n, :]
        wk_all = jnp.exp(r - mm_last)
        m_scr[...] = a[n - 1:n, :] + mm_last
        w_inter_q = w_inter * QK_SCALE
        mm2 = mm * LOG2E
        r2_t = r2_t_all[:, rows]

        def q_of(h):
            return qk_ref[rows, h * ML_DQK:(h + 1) * ML_DQK]

        def k_of(h):
            return qk_ref[rows, ML_HEADS * ML_DQK + h * ML_DQK:ML_HEADS * ML_DQK + (h + 1) * ML_DQK]

        def v_of(h):
            return v_ref[rows, h * ML_DV:(h + 1) * ML_DV]

        ones = jnp.ones((n, LANES), BF16)
        for h in range(ML_HEADS):
            q = q_of(h)
            v_ext = jnp.concatenate([v_of(h), ones], axis=1)
            cn_prev = cn_scr[h]
            w_in = w_inter_q[:, h:h + 1]

            w_intra = jnp.exp2(jnp.where(causal, r2_t[h:h + 1, :] - mm2[:, h:h + 1], -jnp.inf))
            sc = lax.dot_general(q, k_of(h), (((1,), (1,)), ((), ())),
                                 preferred_element_type=F32) * w_intra
            numden = (jnp.dot(sc.astype(BF16), v_ext, preferred_element_type=F32)
                      + w_in * jnp.dot(q, cn_prev.astype(BF16), preferred_element_type=F32))
            den = numden[:, ML_DV:]
            inv = 1.0 / jnp.maximum(jnp.abs(den), floor[:, h:h + 1])
            hval = numden[:, :ML_DV] * jnp.concatenate([inv] * (ML_DV // LANES), axis=1)
            h_ref[rows, h * ML_DV:(h + 1) * ML_DV] = hval.astype(h_ref.dtype)

            decay = w_inter[n - 1:n, h:h + 1]
            kw = k_of(h).astype(F32) * wk_all[:, h:h + 1]
            cn_scr[h] = decay * cn_prev + jnp.dot(kw.T.astype(BF16), v_ext,
                                                  preferred_element_type=F32)


def _ssd_kernel(xs_ref, bc_ref, g_ref, acoef_ref, dskip_ref, y_ref, st_scr):
    @pl.when(pl.program_id(1) == 0)
    def _():
        st_scr[...] = jnp.zeros_like(st_scr)

    ngs = SSM_GROUPS * SSM_STATE
    pair_w = 2 * SSM_HEADDIM
    nrows = xs_ref.shape[0]
    nch = nrows // CHUNK
    dt_all = _softplus(g_ref[:, :LANES])
    a_all = _cumsum_rows(dt_all * acoef_ref[...], _lower_tri(nrows, CHUNK)) * LOG2E
    src_all = a_all - jnp.log2(dt_all)

    def doubled_t(x):
        x3 = x.reshape(nch, CHUNK, LANES)
        return jnp.concatenate([x3, x3], axis=1).reshape(2 * nrows, LANES).T

    src2_t_all = doubled_t(src_all)
    lane = lax.broadcasted_iota(jnp.int32, (1, pair_w), 1)
    left = lane < SSM_HEADDIM
    left_bf = left.astype(BF16)
    right_bf = (lane >= SSM_HEADDIM).astype(BF16)
    row = lax.broadcasted_iota(jnp.int32, (CHUNK, 2 * CHUNK), 0)
    col = lax.broadcasted_iota(jnp.int32, (CHUNK, 2 * CHUNK), 1)
    causal2 = jnp.bitwise_and(col, CHUNK - 1) <= row
    eye2 = jnp.bitwise_and(col, CHUNK - 1) == row

    for ch in range(nch):
        rows = slice(ch * CHUNK, (ch + 1) * CHUNK)
        a = a_all[rows, :]
        src2_t = src2_t_all[:, ch * 2 * CHUNK:(ch + 1) * 2 * CHUNK]

        for gi in range(SSM_GROUPS):
            bg = bc_ref[rows, gi * SSM_STATE:(gi + 1) * SSM_STATE]
            cg = bc_ref[rows, ngs + gi * SSM_STATE:ngs + (gi + 1) * SSM_STATE]
            b2 = jnp.concatenate([bg, bg], axis=0)
            scores2 = lax.dot_general(cg, b2, (((1,), (1,)), ((), ())),
                                      preferred_element_type=F32)
            st = st_scr[gi]
            y_inter = jnp.dot(cg, st.astype(BF16), preferred_element_type=F32)
            bg_t = bg.astype(F32).T.astype(BF16)
            for p in range(HEADS_PER_GROUP // 2):
                pair = gi * (HEADS_PER_GROUP // 2) + p
                la = GATE_DT + 2 * pair
                lb = la + 1
                slab = slice(pair * pair_w, (pair + 1) * pair_w)
                sl_g = slice(p * pair_w, (p + 1) * pair_w)
                a_col2 = jnp.where(left, a[:, la:la + 1], a[:, lb:lb + 1])
                src_row2 = jnp.where(left, src2_t[la:la + 1, :], src2_t[lb:lb + 1, :])
                a_last2 = a_col2[CHUNK - 1:CHUNK, :]

                decay_dt2 = jnp.exp2(jnp.where(causal2, a_col2 - src_row2, -jnp.inf))
                mix2 = (decay_dt2 * scores2).astype(BF16)
                wts_row2 = jnp.exp2(a_last2 - src_row2)
                diag2 = jnp.where(eye2, wts_row2, 0.0).astype(BF16)
                xp_bf = xs_ref[rows, slab]
                xp_bd = jnp.concatenate([xp_bf * left_bf, xp_bf * right_bf], axis=0)
                both = jnp.dot(jnp.concatenate([mix2, diag2], axis=0), xp_bd,
                               preferred_element_type=F32)
                y = (both[:CHUNK, :]
                     + jnp.exp2(a_col2) * y_inter[:, sl_g]
                     + dskip_ref[:, slab] * xp_bf.astype(F32))
                y_ref[rows, slab] = y.astype(y_ref.dtype)

                xw = both[CHUNK:, :].astype(BF16)
                st_scr[gi, :, sl_g] = (jnp.exp2(a_last2) * st[:, sl_g]
                                       + jnp.dot(bg_t, xw, preferred_element_type=F32))


def _scans_kernel(qk_ref, v_ref, g_ref, xs_ref, bc_ref, acoef_ref, dskip_ref, h_ref, y_ref,
                  cn_scr, m_scr, st_scr):
    _mlstm_kernel(qk_ref, v_ref, g_ref, h_ref, cn_scr, m_scr)
    _ssd_kernel(xs_ref, bc_ref, g_ref, acoef_ref, dskip_ref, y_ref, st_scr)


def _scans(conv_out, plain_out, gates, acoef_row, dskip_row, *, rows):
    b, s, _ = conv_out.shape
    blk = lambda bi, c: (bi, c, 0)
    return pl.pallas_call(
        _scans_kernel,
        grid=(b, s // rows),
        in_specs=[
            pl.BlockSpec((None, rows, SEG_W), lambda bi, c: (bi, c, CONV_QK)),
            pl.BlockSpec((None, rows, SEG_W), lambda bi, c: (bi, c, PLAIN_V)),
            pl.BlockSpec((None, rows, GATE_W), blk),
            pl.BlockSpec((None, rows, SEG_W), lambda bi, c: (bi, c, CONV_XS)),
            pl.BlockSpec((None, rows, SSM_BC), lambda bi, c: (bi, c, CONV_BC_BLOCK)),
            pl.BlockSpec((1, LANES), lambda bi, c: (0, 0)),
            pl.BlockSpec((1, SSM_INNER), lambda bi, c: (0, 0)),
        ],
        out_specs=[pl.BlockSpec((None, rows, ML_INNER), blk),
                   pl.BlockSpec((None, rows, SSM_INNER), blk)],
        out_shape=[jax.ShapeDtypeStruct((b, s, ML_INNER), BF16),
                   jax.ShapeDtypeStruct((b, s, SSM_INNER), BF16)],
        scratch_shapes=[pltpu.VMEM((ML_HEADS, ML_DQK, ML_DV + LANES), F32),
                        pltpu.VMEM((1, LANES), F32),
                        pltpu.VMEM((SSM_GROUPS, SSM_STATE, HEADS_PER_GROUP * SSM_HEADDIM), F32)],
        compiler_params=pltpu.CompilerParams(dimension_semantics=("parallel", "arbitrary"),
                                             vmem_limit_bytes=VMEM_LIMIT),
        name="scans",
    )(conv_out, plain_out, gates, conv_out, conv_out, acoef_row, dskip_row)


def _outproj_kernel(h_ref, o_ref, zm_ref, ys_ref, zs_ref, mg_ref, x_ref, gate_ref,
                    wm_ref, ws_ref, wo_ref, fw_ref, out_ref):
    ym = []
    for hd in range(ML_HEADS):
        sl = slice(hd * ML_DV, (hd + 1) * ML_DV)
        hh = h_ref[:, sl].astype(F32)
        rs = lax.rsqrt(jnp.mean(hh * hh, axis=-1, keepdims=True) + EPS)
        zh = zm_ref[:, sl].astype(F32)
        gates = (jnp.tanh(o_ref[:, sl].astype(F32)) + 1.0) * (jnp.tanh(zh) + 1.0)
        ym.append((gates * (zh * hh) * rs).astype(BF16))
    pm = jnp.dot(jnp.concatenate(ym, axis=1), wm_ref[...], preferred_element_type=F32)
    yn = []
    gw = SSM_INNER // SSM_GROUPS
    for gi in range(SSM_GROUPS):
        sl = slice(gi * gw, (gi + 1) * gw)
        zh = zs_ref[:, sl].astype(F32)
        t = ys_ref[:, sl].astype(F32) * zh * (jnp.tanh(zh) + 1.0)
        rs = lax.rsqrt(jnp.mean(t * t, axis=-1, keepdims=True) + EPS)
        yn.append((t * rs).astype(BF16))
    ps = jnp.dot(jnp.concatenate(yn, axis=1), ws_ref[...], preferred_element_type=F32)
    gm = jnp.tanh(mg_ref[:, :D_MODEL].astype(F32)) + 1.0
    gs = jnp.tanh(mg_ref[:, D_MODEL:].astype(F32)) + 1.0
    merged = (gm * pm + gs * ps).astype(BF16)
    r = jnp.dot(merged, wo_ref[...], preferred_element_type=F32)
    xo = x_ref[...] + gate_ref[...] * r
    ms = jnp.mean(xo * xo, axis=-1, keepdims=True)
    out_ref[...] = xo * lax.rsqrt(ms + EPS) * fw_ref[...]


def _outproj(h_m, y_s, plain_out, x, gate, w_m, w_s, w_o, final_w, *, tm):
    b, s, _ = x.shape

    def seg(k):
        return pl.BlockSpec((None, tm, SEG_W), lambda bi, i: (bi, i, k))

    def full(shape):
        return pl.BlockSpec(shape, lambda bi, i: (0,) * len(shape), pipeline_mode=pl.Buffered(1))

    return pl.pallas_call(
        _outproj_kernel,
        grid=(b, s // tm),
        in_specs=[
            pl.BlockSpec((None, tm, ML_INNER), lambda bi, i: (bi, i, 0)),
            seg(PLAIN_O), seg(PLAIN_ZM),
            pl.BlockSpec((None, tm, SSM_INNER), lambda bi, i: (bi, i, 0)),
            seg(PLAIN_ZS), seg(PLAIN_MERGE),
            pl.BlockSpec((None, tm, D_MODEL), lambda bi, i: (bi, i, 0)),
            pl.BlockSpec((None, 1, D_MODEL), lambda bi, i: (bi, 0, 0)),
            full((ML_INNER, D_MODEL)), full((SSM_INNER, D_MODEL)), full((D_MODEL, D_MODEL)),
            full((1, D_MODEL)),
        ],
        out_specs=pl.BlockSpec((None, tm, D_MODEL), lambda bi, i: (bi, i, 0)),
        out_shape=jax.ShapeDtypeStruct((b, s, D_MODEL), F32),
        compiler_params=pltpu.CompilerParams(dimension_semantics=("parallel", "parallel"),
                                             vmem_limit_bytes=VMEM_LIMIT),
        name="outproj",
    )(h_m, plain_out, plain_out, y_s, plain_out, plain_out, x, gate, w_m, w_s, w_o, final_w)


def _seg(w, k):
    return w[..., SEG_OFF[k]:SEG_OFF[k] + SEG_SIZES[k]]


def _gate_cols(i_cols, f_cols, dt_cols):
    lead = i_cols.shape[:-1]

    def z(n):
        return jnp.zeros(lead + (n,), i_cols.dtype)

    return jnp.concatenate(
        [i_cols, z(GATE_DT - ML_HEADS), dt_cols, z(LANES - GATE_DT - SSM_HEADS),
         f_cols, z(LANES - ML_HEADS)], axis=-1)


def _layer(x, c_pad, norm_w, ada_w, ada_b, w_in, b_in, ml_conv_w, ml_conv_b, ml_norm_w,
           ssm_conv_w, ssm_conv_b, ssm_a_log, ssm_d, ssm_norm_w, w_proj_m, w_proj_s, w_out,
           final_w):
    b, s, _ = x.shape
    mod = _mod(c_pad, ada_w, ada_b.reshape(1, -1))[:b]
    shift = mod[:, :D_MODEL].reshape(b, 1, D_MODEL)
    scale = mod[:, D_MODEL:2 * D_MODEL].reshape(b, 1, D_MODEL)
    gate = mod[:, 2 * D_MODEL:].reshape(b, 1, D_MODEL)
    norm_w = norm_w.reshape(1, D_MODEL)

    w_t = w_in.T

    def seg_t(k):
        return w_t[SEG_OFF[k]:SEG_OFF[k] + SEG_SIZES[k], :]

    w_conv = jnp.concatenate([seg_t(0).astype(BF16), seg_t(6).astype(BF16)], axis=0).T
    b_conv = jnp.concatenate([_seg(b_in, 0), _seg(b_in, 6)]).reshape(1, CONV_WIDTH)
    conv_w = jnp.concatenate([ml_conv_w, ssm_conv_w], axis=1)
    conv_b = jnp.concatenate([ml_conv_b, ssm_conv_b]).reshape(1, CONV_WIDTH)
    conv_b_half = 0.5 * (conv_b + b_conv * jnp.sum(conv_w, axis=0, keepdims=True))
    conv_w_half = 0.5 * conv_w
    w_plain = jnp.concatenate(
        [seg_t(1).astype(BF16)] + [(0.5 * seg_t(k)).astype(BF16) for k in (2, 3, 7, 9)], axis=0).T
    b_plain = jnp.concatenate([_seg(b_in, 1), 0.5 * _seg(b_in, 2), 0.5 * _seg(b_in, 3),
                               0.5 * _seg(b_in, 7), 0.5 * _seg(b_in, 9)]).reshape(1, PLAIN_WIDTH)
    w_gate = _gate_cols(_seg(w_in, 4), _seg(w_in, 5), _seg(w_in, 8)).T
    b_gate = _gate_cols(_seg(b_in, 4), _seg(b_in, 5), _seg(b_in, 8)).reshape(1, GATE_W)

    conv_out, plain_out, gates = _inproj(
        x, norm_w, scale, shift, w_conv, b_conv, conv_w_half, conv_b_half,
        w_plain, b_plain, w_gate, b_gate, tm=min(1024, s), ncol=5)

    acoef_row = jnp.pad(-jnp.exp(ssm_a_log.astype(F32)),
                        (GATE_DT, LANES - GATE_DT - SSM_HEADS)).reshape(1, LANES)
    dskip_row = jnp.repeat(ssm_d, SSM_HEADDIM).reshape(1, SSM_INNER)
    h_m, y_s = _scans(conv_out, plain_out, gates, acoef_row, dskip_row, rows=min(SCAN_ROWS, s))

    w_m = ((0.5 * ml_norm_w)[:, None] * w_proj_m).astype(BF16)
    w_s = (ssm_norm_w[:, None] * w_proj_s).astype(BF16)
    w_o = (0.5 * w_out).astype(BF16)
    return _outproj(h_m, y_s, plain_out, x, gate, w_m, w_s, w_o, final_w.reshape(1, D_MODEL),
                    tm=min(512, s))


def kernel(x, c, norm_w, ada_w, ada_b, w_in, b_in, ml_conv_w, ml_conv_b, ml_norm_w,
           ssm_conv_w, ssm_conv_b, ssm_a_log, ssm_d, ssm_norm_w, w_proj_m, w_proj_s, w_out,
           final_w):
    b = x.shape[0]
    assert norm_w.shape[0] == 1, "the residual + final-norm epilogue is fused for a single layer"
    c_pad = jnp.pad(c, ((0, (-b) % SUBLANES), (0, 0)))
    return _layer(x, c_pad, norm_w[0], ada_w[0], ada_b[0], w_in[0], b_in[0], ml_conv_w[0],
                  ml_conv_b[0], ml_norm_w[0], ssm_conv_w[0], ssm_conv_b[0], ssm_a_log[0],
                  ssm_d[0], ssm_norm_w[0], w_proj_m[0], w_proj_s[0], w_out[0], final_w)
```
